```python
import jax, jax.numpy as jnp
from jax import lax
import numpy as np

D_MODEL = 1024
BATCH = 8
SEQ = 4096
DEPTH = 2

N_META = 16
N_MIXERS = 2
RMS_EPS = 1e-6

MLA_HEADS = 8
QK_NOPE = 128
QK_ROPE = 64
V_HEAD = 128
Q_LORA = 384
KV_LORA = 256
MLA_WIDTH = MLA_HEADS * V_HEAD
ROPE_BASE = 10000.0
Q_BLOCK = 128
MASK_VALUE = -1e30

LRU_WIDTH = 1024
LRU_BLOCKS = 4
LRU_BLOCK = LRU_WIDTH // LRU_BLOCKS
CONV_WIDTH = 4
LRU_C = 8.0

kernel_name = 'mla_rglru_interleaved_hybrid'


def rmsnorm(x, g):
    xf = x.astype(jnp.float32)
    y = xf * lax.rsqrt(jnp.mean(xf * xf, axis=-1, keepdims=True) + RMS_EPS)
    return (y * g.astype(jnp.float32)).astype(x.dtype)


def rotate_half_split(x, cos, sin):
    x1, x2 = jnp.split(x, 2, axis=-1)
    return jnp.concatenate([x1 * cos - x2 * sin, x1 * sin + x2 * cos], axis=-1).astype(x.dtype)


def block_causal_attention(q_nope, q_rope, k_nope, k_rope, v):
    B, T, H, _ = q_nope.shape
    pad = (-T) % Q_BLOCK
    Tp = T + pad
    nb = Tp // Q_BLOCK

    def padt(a):
        return jnp.pad(a, [(0, 0), (pad, 0)] + [(0, 0)] * (a.ndim - 2))

    q_nope, q_rope, k_nope, k_rope, v = (padt(a) for a in (q_nope, q_rope, k_nope, k_rope, v))
    scale = (QK_NOPE + QK_ROPE) ** -0.5
    key_idx = jnp.arange(Tp)

    def to_blocks(a):
        return jnp.moveaxis(a.reshape(B, nb, Q_BLOCK, *a.shape[2:]), 1, 0)

    def one_block(args):
        blk, qn, qr = args
        s = (jnp.einsum('bqhd,bkhd->bhqk', qn, k_nope, preferred_element_type=jnp.float32)
             + jnp.einsum('bqhr,bkr->bhqk', qr, k_rope, preferred_element_type=jnp.float32)) * scale
        q_idx = blk * Q_BLOCK + jnp.arange(Q_BLOCK)
        mask = (key_idx[None, :] <= q_idx[:, None]) & (key_idx[None, :] >= pad)
        s = jnp.where(mask[None, None], s, MASK_VALUE)
        p = jax.nn.softmax(s, axis=-1)
        return jnp.einsum('bhqk,bkhd->bqhd', p.astype(v.dtype), v)

    out = lax.map(one_block, (jnp.arange(nb), to_blocks(q_nope), to_blocks(q_rope)))
    out = jnp.moveaxis(out, 0, 1).reshape(B, Tp, H, V_HEAD)
    return out[:, pad:]


def mla_mixer(h, w_in, q_norm_g, kv_norm_g, w_uq, w_ukv, w_out):
    B, T, _ = h.shape
    proj = h @ w_in
    q_lat, kv_lat, k_rope, gate = jnp.split(
        proj, [Q_LORA, Q_LORA + KV_LORA, Q_LORA + KV_LORA + QK_ROPE], axis=-1)
    q = (rmsnorm(q_lat, q_norm_g) @ w_uq).reshape(B, T, MLA_HEADS, QK_NOPE + QK_ROPE)
    q_nope, q_rope = q[..., :QK_NOPE], q[..., QK_NOPE:]
    kv = (rmsnorm(kv_lat, kv_norm_g) @ w_ukv).reshape(B, T, MLA_HEADS, QK_NOPE + V_HEAD)
    k_nope, v = kv[..., :QK_NOPE], kv[..., QK_NOPE:]
    pos = jnp.arange(T, dtype=jnp.float32)
    inv_freq = ROPE_BASE ** (-jnp.arange(0, QK_ROPE, 2, dtype=jnp.float32) / QK_ROPE)
    ang = pos[:, None] * inv_freq[None, :]
    cos, sin = jnp.cos(ang), jnp.sin(ang)
    q_rope = rotate_half_split(q_rope, cos[:, None, :], sin[:, None, :])
    k_rope = rotate_half_split(k_rope, cos, sin)
    attn = block_causal_attention(q_nope, q_rope, k_nope, k_rope, v)
    y = attn.reshape(B, T, MLA_WIDTH) * jax.nn.silu(gate)
    return y @ w_out


def rglru_mixer(h, w_in, conv_w, conv_b, w_rg, b_rg, w_ig, b_ig, lam, w_out):
    B, T, _ = h.shape
    proj = h @ w_in
    u, gate = jnp.split(proj, [LRU_WIDTH], axis=-1)
    up = jnp.pad(u, ((0, 0), (CONV_WIDTH - 1, 0), (0, 0)))
    uc = conv_b + up[:, 0:T] * conv_w[0]
    for j in range(1, CONV_WIDTH):
        uc = uc + up[:, j:j + T] * conv_w[j]
    ub = uc.reshape(B, T, LRU_BLOCKS, LRU_BLOCK)
    r = jax.nn.sigmoid(jnp.einsum('btgi,gij->btgj', ub, w_rg).reshape(B, T, LRU_WIDTH) + b_rg)
    i = jax.nn.sigmoid(jnp.einsum('btgi,gij->btgj', ub, w_ig).reshape(B, T, LRU_WIDTH) + b_ig)
    log_a = -LRU_C * r.astype(jnp.float32) * jax.nn.softplus(-lam.astype(jnp.float32))
    a = jnp.exp(log_a)
    mult = jnp.sqrt(-jnp.expm1(2.0 * log_a))
    mult = jnp.where(jnp.arange(T)[None, :, None] == 0, 1.0, mult)
    b = mult * (i * uc).astype(jnp.float32)

    def combine(left, right):
        a1, b1 = left
        a2, b2 = right
        return a1 * a2, a2 * b1 + b2

    _, hs = lax.associative_scan(combine, (a, b), axis=1)
    y = hs.astype(h.dtype) * jax.nn.silu(gate)
    return y @ w_out


def setup_inputs(seed: int = 0) -> dict:
    key = jax.random.key(seed)
    ks = jax.random.split(key, 24)
    n_a = (DEPTH + 1) // 2
    n_b = DEPTH // 2
    d = D_MODEL
    f32 = jnp.float32

    def nrm(k, shape, fan_in):
        return jax.random.normal(k, shape, f32) * (fan_in ** -0.5)

    def gain(k, shape):
        return 1.0 + 0.01 * jax.random.normal(k, shape, f32)

    a_in_cols = Q_LORA + KV_LORA + QK_ROPE + MLA_WIDTH
    u0 = jax.random.uniform(ks[17], (n_b, LRU_WIDTH), f32, minval=0.9, maxval=0.999)
    s0 = u0 ** (1.0 / LRU_C)
    lam = jnp.log(s0) - jnp.log1p(-s0)
    return {
        'x': jax.random.normal(ks[0], (BATCH, SEQ, d), f32),
        'meta_tokens': jax.random.normal(ks[1], (N_META, d), f32),
        'a_norm_g': gain(ks[2], (n_a, d)),
        'a_w_in': nrm(ks[3], (n_a, d, a_in_cols), d),
        'a_q_norm_g': gain(ks[4], (n_a, Q_LORA)),
        'a_kv_norm_g': gain(ks[5], (n_a, KV_LORA)),
        'a_w_uq': nrm(ks[6], (n_a, Q_LORA, MLA_HEADS * (QK_NOPE + QK_ROPE)), Q_LORA),
        'a_w_ukv': nrm(ks[7], (n_a, KV_LORA, MLA_HEADS * (QK_NOPE + V_HEAD)), KV_LORA),
        'a_w_out': nrm(ks[8], (n_a, MLA_WIDTH, d), MLA_WIDTH),
        'b_norm_g': gain(ks[9], (n_b, d)),
        'b_w_in': nrm(ks[10], (n_b, d, 2 * LRU_WIDTH), d),
        'b_conv_w': nrm(ks[11], (n_b, CONV_WIDTH, LRU_WIDTH), CONV_WIDTH),
        'b_conv_b': 0.01 * jax.random.normal(ks[12], (n_b, LRU_WIDTH), f32),
        'b_w_rg': nrm(ks[13], (n_b, LRU_BLOCKS, LRU_BLOCK, LRU_BLOCK), LRU_BLOCK),
        'b_b_rg': 0.01 * jax.random.normal(ks[14], (n_b, LRU_WIDTH), f32),
        'b_w_ig': nrm(ks[15], (n_b, LRU_BLOCKS, LRU_BLOCK, LRU_BLOCK), LRU_BLOCK),
        'b_b_ig': 0.01 * jax.random.normal(ks[16], (n_b, LRU_WIDTH), f32),
        'b_lam': lam,
        'b_w_out': nrm(ks[18], (n_b, LRU_WIDTH, d), LRU_WIDTH),
        'final_norm_g': gain(ks[19], (d,)),
    }


def reference(x, meta_tokens, a_norm_g, a_w_in, a_q_norm_g, a_kv_norm_g, a_w_uq, a_w_ukv,
              a_w_out, b_norm_g, b_w_in, b_conv_w, b_conv_b, b_w_rg, b_b_rg, b_w_ig, b_b_ig,
              b_lam, b_w_out, final_norm_g):
    B = x.shape[0]
    meta = jnp.broadcast_to(meta_tokens[None].astype(x.dtype), (B, N_META, x.shape[-1]))
    h = jnp.concatenate([meta, x], axis=1)
    for layer in range(DEPTH):
        j = layer // N_MIXERS
        if layer % N_MIXERS == 0:
            h = h + mla_mixer(rmsnorm(h, a_norm_g[j]), a_w_in[j], a_q_norm_g[j], a_kv_norm_g[j],
                              a_w_uq[j], a_w_ukv[j], a_w_out[j])
        else:
            h = h + rglru_mixer(rmsnorm(h, b_norm_g[j]), b_w_in[j], b_conv_w[j], b_conv_b[j],
                                b_w_rg[j], b_b_rg[j], b_w_ig[j], b_b_ig[j], b_lam[j], b_w_out[j])
    h = rmsnorm(h, final_norm_g)
    return h[:, N_META:]
```

```python
import functools
import math

import jax
import jax.numpy as jnp
from jax import lax
from jax.experimental import pallas as pl
from jax.experimental.pallas import tpu as pltpu

F32 = jnp.float32
BF16 = jnp.bfloat16

D_MODEL = 1024
N_META = 16
RMS_EPS = 1e-6
HEADS = 8
QK_NOPE = 128
QK_ROPE = 64
V_HEAD = 128
Q_LORA = 384
KV_LORA = 256
ROPE_BASE = 10000.0
HEAD_PAD = 256
ROPE_LANES = 128
PROJ_COLS = Q_LORA + KV_LORA + ROPE_LANES
LRU_WIDTH = 1024
LRU_BLOCKS = 4
LRU_BLOCK = 256
CONV_WIDTH = 4
LRU_C = 8.0
MASK_VALUE = -1e30
LOG2E = math.log2(math.e)

VMEM_LIMIT_BYTES = 56 * 1024 * 1024

PROJ_ROWS = 512
ATTN_Q = 512
ATTN_K = 512
TAIL_ROWS = 256
CARRY_ROWS = 8


def _rmsnorm(x, g):
    ms = jnp.mean(x * x, axis=-1, keepdims=True)
    return (x * lax.rsqrt(ms + RMS_EPS)) * g


def _rope128(x, c, sa, sb):
    return x * c + pltpu.roll(x, 96, 1) * sa + pltpu.roll(x, 32, 1) * sb


def _silu(x):
    return x * jax.nn.sigmoid(x)


def _dot(a, b):
    return jnp.dot(a, b, preferred_element_type=F32)


def _dot_nt(a, b):
    return lax.dot_general(a, b, (((1,), (1,)), ((), ())), preferred_element_type=F32)


def _mla_latents(hn_bf16, w1, qg, kvg, wukv, c, sa, sb):
    proj = _dot(hn_bf16, w1)
    qln = _rmsnorm(proj[:, :Q_LORA], qg).astype(BF16)
    kvn = _rmsnorm(proj[:, Q_LORA:Q_LORA + KV_LORA], kvg).astype(BF16)
    kv = _dot(kvn, wukv)
    k_nope = kv[:, :HEADS * QK_NOPE].astype(BF16)
    v = kv[:, HEADS * QK_NOPE:].astype(BF16)
    k_rope = _rope128(proj[:, Q_LORA + KV_LORA:], c, sa, sb).astype(BF16)
    return qln, k_nope, v, k_rope


def _q_head(qln, wuq_h, c, sa, sb):
    q = _dot(qln, wuq_h)
    q_rope = _rope128(q[:, QK_NOPE:], c, sa, sb)
    return jnp.concatenate([q[:, :QK_NOPE], q_rope], axis=1).astype(BF16)


def _lru_gates(uc, wgate_ref, b_rg, b_ig, lam):
    rs, is_ = [], []
    for g in range(LRU_BLOCKS):
        ri = _dot(uc[:, g * LRU_BLOCK:(g + 1) * LRU_BLOCK].astype(BF16), wgate_ref[g])
        rs.append(ri[:, :LRU_BLOCK])
        is_.append(ri[:, LRU_BLOCK:])
    r = jax.nn.sigmoid(jnp.concatenate(rs, axis=1) + b_rg)
    i = jax.nn.sigmoid(jnp.concatenate(is_, axis=1) + b_ig)
    neg_lam = -lam
    softplus = jnp.maximum(neg_lam, 0.0) + jnp.log1p(jnp.exp(-jnp.abs(neg_lam)))
    log_a = (-LRU_C) * r * softplus
    a = jnp.exp(log_a)
    mult = jnp.sqrt(jnp.tanh(-log_a) * (1.0 + a * a))
    return a, mult, i * uc


def _meta_kernel(meta_ref, ag_ref, w1_ref, wg_ref, qg_ref, kvg_ref, wuq_ref, wukv_ref, wao_ref,
                 bg_ref, wbu_ref, cw_ref, cb_ref, wgate_ref, brg_ref, big_ref, lam_ref,
                 c_ref, sa_ref, sb_ref,
                 kmeta_ref, vmeta_ref, utail_ref, h0_ref):
    x = meta_ref[...]
    c, sa, sb = c_ref[...], sa_ref[...], sb_ref[...]
    hn = _rmsnorm(x, ag_ref[...]).astype(BF16)
    qln, k_nope, v, k_rope = _mla_latents(hn, w1_ref[...], qg_ref[...], kvg_ref[...],
                                           wukv_ref[...], c, sa, sb)
    gate = _dot(hn, wg_ref[...])
    row = lax.broadcasted_iota(jnp.int32, (N_META, N_META), 0)
    col = lax.broadcasted_iota(jnp.int32, (N_META, N_META), 1)
    outs = []
    for h in range(HEADS):
        k_h = jnp.concatenate([k_nope[:, h * QK_NOPE:(h + 1) * QK_NOPE], k_rope], axis=1)
        v_h = v[:, h * V_HEAD:(h + 1) * V_HEAD]
        kmeta_ref[h] = k_h
        vmeta_ref[h] = v_h
        q_h = _q_head(qln, wuq_ref[h], c, sa, sb)
        s = _dot_nt(q_h, k_h)
        s = jnp.where(col <= row, s, MASK_VALUE)
        p = jnp.exp2(s - jnp.max(s, axis=1, keepdims=True))
        p = p / jnp.sum(p, axis=1, keepdims=True)
        outs.append(_dot(p.astype(BF16), v_h))
    attn = jnp.concatenate(outs, axis=1)
    y = (attn * _silu(gate)).astype(BF16)
    h1 = x + _dot(y, wao_ref[...])

    hn1 = _rmsnorm(h1, bg_ref[...]).astype(BF16)
    u = _dot(hn1, wbu_ref[...])
    trow = lax.broadcasted_iota(jnp.int32, (N_META, LRU_WIDTH), 0)
    uc = cb_ref[...] + u * cw_ref[CONV_WIDTH - 1:CONV_WIDTH, :]
    for k in range(1, CONV_WIDTH):
        shifted = jnp.where(trow >= k, pltpu.roll(u, k, 0), 0.0)
        uc = uc + shifted * cw_ref[CONV_WIDTH - 1 - k:CONV_WIDTH - k, :]
    a, mult, iu = _lru_gates(uc, wgate_ref, brg_ref[...], big_ref[...], lam_ref[...])
    b = jnp.where(trow == 0, 1.0, mult) * iu
    h = jnp.zeros((1, LRU_WIDTH), F32)
    for t in range(N_META):
        h = a[t:t + 1, :] * h + b[t:t + 1, :]
    h0_ref[...] = h
    utail_ref[...] = u[N_META - CARRY_ROWS:, :]


def _proj_kernel(x_ref, ag_ref, w1_ref, qg_ref, kvg_ref, wukv_ref, c_ref, sa_ref, sb_ref,
                 qln_ref, kn_ref, v_ref, kr_ref):
    hn = _rmsnorm(x_ref[...], ag_ref[...]).astype(BF16)
    qln, k_nope, v, k_rope = _mla_latents(hn, w1_ref[...], qg_ref[...], kvg_ref[...],
                                           wukv_ref[...], c_ref[...], sa_ref[...], sb_ref[...])
    qln_ref[...] = qln
    kn_ref[...] = k_nope
    v_ref[...] = v
    kr_ref[...] = k_rope


def _attn_kernel(qln_ref, wuq_ref, c_ref, sa_ref, sb_ref, kn_ref, kr_ref, v_ref,
                 kmeta_ref, vmeta_ref, o_ref, kcat_ref, m_ref, l_ref, acc_ref):
    qi = pl.program_id(2)

    @pl.when(qi == 0)
    def _():
        kcat_ref[:, :QK_NOPE] = kn_ref[...]
        kcat_ref[:, QK_NOPE:] = kr_ref[...]

    q = _q_head(qln_ref[...], wuq_ref[...], c_ref[...], sa_ref[...], sb_ref[...])

    s0 = _dot_nt(q, kmeta_ref[...])
    m0 = jnp.max(s0, axis=1, keepdims=True)
    p0 = jnp.exp2(s0 - m0)
    m_ref[...] = jnp.broadcast_to(m0, m_ref.shape)
    l_ref[...] = jnp.broadcast_to(jnp.sum(p0, axis=1, keepdims=True), l_ref.shape)
    acc_ref[...] = _dot(p0.astype(BF16), vmeta_ref[...])

    def step(j, masked):
        start = pl.multiple_of(j * ATTN_K, ATTN_K)
        s = _dot_nt(q, kcat_ref[pl.ds(start, ATTN_K), :])
        if masked:
            row = lax.broadcasted_iota(jnp.int32, s.shape, 0)
            col = lax.broadcasted_iota(jnp.int32, s.shape, 1)
            s = jnp.where(col <= row, s, MASK_VALUE)
        m_prev = m_ref[...]
        m_next = jnp.maximum(m_prev, jnp.max(s, axis=1, keepdims=True))
        p = jnp.exp2(s - pltpu.repeat(m_next, ATTN_K // 128, axis=1))
        alpha = jnp.exp2(m_prev - m_next)
        l_ref[...] = alpha * l_ref[...] + jnp.sum(p, axis=1, keepdims=True)
        acc_ref[...] = alpha * acc_ref[...] + _dot(p.astype(BF16), v_ref[pl.ds(start, ATTN_K), :])
        m_ref[...] = m_next

    def body(j, carry):
        step(j, masked=False)
        return carry

    lax.fori_loop(0, qi, body, 0)
    step(qi, masked=True)
    o_ref[...] = (acc_ref[...] / l_ref[...]).astype(o_ref.dtype)


def _tail_kernel(x_ref, attn_ref, ag_ref, wg_ref, wao_ref, bg_ref, wbin_ref, cw_ref, cb_ref,
                 wgate_ref, brg_ref, big_ref, lam_ref, wbo_ref, fg_ref, utail_ref, h0_ref,
                 o_ref, ubuf_ref, a_ref, b_ref, hs_ref, hcar_ref):
    ti = pl.program_id(1)

    @pl.when(ti == 0)
    def _():
        ubuf_ref[:CARRY_ROWS, :] = utail_ref[...]
        hcar_ref[...] = h0_ref[...]

    x = x_ref[...]
    hn = _rmsnorm(x, ag_ref[...]).astype(BF16)
    gate = _dot(hn, wg_ref[...])
    y = (attn_ref[...].astype(F32) * _silu(gate)).astype(BF16)
    h1 = x + _dot(y, wao_ref[...])

    hn1 = _rmsnorm(h1, bg_ref[...]).astype(BF16)
    proj = _dot(hn1, wbin_ref[...])
    u = proj[:, :LRU_WIDTH]
    gate_b = proj[:, LRU_WIDTH:]
    ubuf_ref[CARRY_ROWS:, :] = u
    uc = cb_ref[...] + u * cw_ref[CONV_WIDTH - 1:CONV_WIDTH, :]
    for k in range(1, CONV_WIDTH):
        uc = uc + ubuf_ref[CARRY_ROWS - k:CARRY_ROWS - k + TAIL_ROWS, :] * cw_ref[CONV_WIDTH - 1 - k:CONV_WIDTH - k, :]
    ubuf_ref[:CARRY_ROWS, :] = ubuf_ref[TAIL_ROWS:, :]

    a, mult, iu = _lru_gates(uc, wgate_ref, brg_ref[...], big_ref[...], lam_ref[...])
    a_ref[...] = a
    b_ref[...] = mult * iu

    def scan(t, h):
        h = a_ref[pl.ds(t, 1), :] * h + b_ref[pl.ds(t, 1), :]
        hs_ref[pl.ds(t, 1), :] = h
        return h

    hcar_ref[...] = lax.fori_loop(0, TAIL_ROWS, scan, hcar_ref[...], unroll=8)

    y2 = (hs_ref[...] * _silu(gate_b)).astype(BF16)
    h2 = h1 + _dot(y2, wbo_ref[...])
    o_ref[...] = _rmsnorm(h2, fg_ref[...])


def _const_spec(shape, grid_rank):
    zeros = (0,) * len(shape)
    return pl.BlockSpec(shape, lambda *_: zeros, pipeline_mode=pl.Buffered(1))


def _rope_tables(n_pos):
    pos = jnp.arange(n_pos, dtype=F32)
    inv_freq = ROPE_BASE ** (-jnp.arange(0, QK_ROPE, 2, dtype=F32) / QK_ROPE)
    ang = pos[:, None] * inv_freq[None, :]
    cos, sin = jnp.cos(ang), jnp.sin(ang)
    z = jnp.zeros_like(cos)
    c = jnp.concatenate([cos, cos, z, z], axis=1)
    sa = jnp.concatenate([-sin, z, z, z], axis=1)
    sb = jnp.concatenate([z, sin, z, z], axis=1)
    return c, sa, sb


def kernel(x, meta_tokens, a_norm_g, a_w_in, a_q_norm_g, a_kv_norm_g, a_w_uq, a_w_ukv, a_w_out,
           b_norm_g, b_w_in, b_conv_w, b_conv_b, b_w_rg, b_b_rg, b_w_ig, b_b_ig, b_lam, b_w_out,
           final_norm_g):
    B, T, D = x.shape
    assert D == D_MODEL and T % ATTN_Q == 0 and T % PROJ_ROWS == 0 and T % TAIL_ROWS == 0
    assert a_norm_g.shape[0] == 1 and b_norm_g.shape[0] == 1

    w_in = a_w_in[0]
    n_lat = Q_LORA + KV_LORA + QK_ROPE
    w1 = jnp.concatenate([w_in[:, :n_lat], jnp.zeros((D, ROPE_LANES - QK_ROPE), F32)], axis=1).astype(BF16)
    wg = w_in[:, n_lat:].astype(BF16)
    wuq = a_w_uq[0].reshape(Q_LORA, HEADS, QK_NOPE + QK_ROPE) * ((QK_NOPE + QK_ROPE) ** -0.5 * LOG2E)
    wuq = jnp.pad(wuq, ((0, 0), (0, 0), (0, HEAD_PAD - QK_NOPE - QK_ROPE)))
    wuq = jnp.transpose(wuq, (1, 0, 2)).astype(BF16)
    wukv = a_w_ukv[0].reshape(KV_LORA, HEADS, QK_NOPE + V_HEAD)
    wukv = jnp.concatenate([wukv[:, :, :QK_NOPE].reshape(KV_LORA, HEADS * QK_NOPE),
                            wukv[:, :, QK_NOPE:].reshape(KV_LORA, HEADS * V_HEAD)], axis=1).astype(BF16)
    wao = a_w_out[0].astype(BF16)
    wbin = b_w_in[0].astype(BF16)
    wgate = jnp.concatenate([b_w_rg[0], b_w_ig[0]], axis=2).astype(BF16)
    wbo = b_w_out[0].astype(BF16)
    row = lambda v: v.reshape(1, -1)
    ag, qg, kvg, bg, fg = row(a_norm_g[0]), row(a_q_norm_g[0]), row(a_kv_norm_g[0]), row(b_norm_g[0]), row(final_norm_g)
    cw, cb = b_conv_w[0], row(b_conv_b[0])
    brg, big, lam = row(b_b_rg[0]), row(b_b_ig[0]), row(b_lam[0])
    c_all, sa_all, sb_all = _rope_tables(N_META + T)
    c_m, sa_m, sb_m = c_all[:N_META], sa_all[:N_META], sb_all[:N_META]
    c_r, sa_r, sb_r = c_all[N_META:], sa_all[N_META:], sb_all[N_META:]

    params = functools.partial(pltpu.CompilerParams, vmem_limit_bytes=VMEM_LIMIT_BYTES)

    kmeta, vmeta, utail, h0 = pl.pallas_call(
        _meta_kernel,
        out_shape=(jax.ShapeDtypeStruct((HEADS, N_META, HEAD_PAD), BF16),
                   jax.ShapeDtypeStruct((HEADS, N_META, V_HEAD), BF16),
                   jax.ShapeDtypeStruct((CARRY_ROWS, LRU_WIDTH), F32),
                   jax.ShapeDtypeStruct((1, LRU_WIDTH), F32)),
        compiler_params=params(),
        name="meta",
    )(meta_tokens, ag, w1, wg, qg, kvg, wuq, wukv, wao, bg, wbin[:, :LRU_WIDTH], cw, cb, wgate,
      brg, big, lam, c_m, sa_m, sb_m)

    n_p = T // PROJ_ROWS
    tile_p = lambda w: pl.BlockSpec((None, PROJ_ROWS, w), lambda b, i: (b, i, 0))
    tab_p = pl.BlockSpec((PROJ_ROWS, ROPE_LANES), lambda b, i: (i, 0))
    qln, k_nope, v, k_rope = pl.pallas_call(
        _proj_kernel,
        grid=(B, n_p),
        in_specs=[tile_p(D), _const_spec((1, D), 2), _const_spec((D, PROJ_COLS), 2),
                  _const_spec((1, Q_LORA), 2), _const_spec((1, KV_LORA), 2),
                  _const_spec((KV_LORA, HEADS * (QK_NOPE + V_HEAD)), 2), tab_p, tab_p, tab_p],
        out_specs=(tile_p(Q_LORA), tile_p(HEADS * QK_NOPE), tile_p(HEADS * V_HEAD), tile_p(ROPE_LANES)),
        out_shape=(jax.ShapeDtypeStruct((B, T, Q_LORA), BF16),
                   jax.ShapeDtypeStruct((B, T, HEADS * QK_NOPE), BF16),
                   jax.ShapeDtypeStruct((B, T, HEADS * V_HEAD), BF16),
                   jax.ShapeDtypeStruct((B, T, ROPE_LANES), BF16)),
        compiler_params=params(dimension_semantics=("parallel", "parallel")),
        name="proj",
    )(x, ag, w1, qg, kvg, wukv, c_r, sa_r, sb_r)

    n_q = T // ATTN_Q
    tab_q = pl.BlockSpec((ATTN_Q, ROPE_LANES), lambda b, h, i: (i, 0))
    attn = pl.pallas_call(
        _attn_kernel,
        grid=(B, HEADS, n_q),
        in_specs=[pl.BlockSpec((None, ATTN_Q, Q_LORA), lambda b, h, i: (b, i, 0)),
                  pl.BlockSpec((None, Q_LORA, HEAD_PAD), lambda b, h, i: (h, 0, 0)),
                  tab_q, tab_q, tab_q,
                  pl.BlockSpec((None, T, QK_NOPE), lambda b, h, i: (b, 0, h)),
                  pl.BlockSpec((None, T, ROPE_LANES), lambda b, h, i: (b, 0, 0)),
                  pl.BlockSpec((None, T, V_HEAD), lambda b, h, i: (b, 0, h)),
                  pl.BlockSpec((None, N_META, HEAD_PAD), lambda b, h, i: (h, 0, 0)),
                  pl.BlockSpec((None, N_META, V_HEAD), lambda b, h, i: (h, 0, 0))],
        out_specs=pl.BlockSpec((None, ATTN_Q, V_HEAD), lambda b, h, i: (b, i, h)),
        out_shape=jax.ShapeDtypeStruct((B, T, HEADS * V_HEAD), BF16),
        scratch_shapes=[pltpu.VMEM((T, HEAD_PAD), BF16),
                        pltpu.VMEM((ATTN_Q, 128), F32),
                        pltpu.VMEM((ATTN_Q, 128), F32),
                        pltpu.VMEM((ATTN_Q, V_HEAD), F32)],
        compiler_params=params(dimension_semantics=("parallel", "parallel", "arbitrary")),
        name="attn",
    )(qln, wuq, c_r, sa_r, sb_r, k_nope, k_rope, v, kmeta, vmeta)

    n_t = T // TAIL_ROWS
    tile_t = pl.BlockSpec((None, TAIL_ROWS, D), lambda b, i: (b, i, 0))
    cs = lambda shape: _const_spec(shape, 2)
    out = pl.pallas_call(
        _tail_kernel,
        grid=(B, n_t),
        in_specs=[tile_t, tile_t, cs((1, D)), cs((D, D)), cs((D, D)), cs((1, D)), cs((D, 2 * LRU_WIDTH)),
                  cs((CONV_WIDTH, LRU_WIDTH)), cs((1, LRU_WIDTH)),
                  cs((LRU_BLOCKS, LRU_BLOCK, 2 * LRU_BLOCK)), cs((1, LRU_WIDTH)), cs((1, LRU_WIDTH)),
                  cs((1, LRU_WIDTH)), cs((LRU_WIDTH, D)), cs((1, D)),
                  cs((CARRY_ROWS, LRU_WIDTH)), cs((1, LRU_WIDTH))],
        out_specs=tile_t,
        out_shape=jax.ShapeDtypeStruct((B, T, D), x.dtype),
        scratch_shapes=[pltpu.VMEM((CARRY_ROWS + TAIL_ROWS, LRU_WIDTH), F32),
                        pltpu.VMEM((TAIL_ROWS, LRU_WIDTH), F32),
                        pltpu.VMEM((TAIL_ROWS, LRU_WIDTH), F32),
                        pltpu.VMEM((TAIL_ROWS, LRU_WIDTH), F32),
                        pltpu.VMEM((1, LRU_WIDTH), F32)],
        compiler_params=params(dimension_semantics=("arbitrary", "arbitrary")),
        name="tail",
    )(x, attn, ag, wg, wao, bg, wbin, cw, cb, wgate, brg, big, lam, wbo, fg, utail, h0)
    return out
```

```python
import functools
import math

import jax
import jax.numpy as jnp
from jax import lax
from jax.experimental import pallas as pl
from jax.experimental.pallas import tpu as pltpu

F32 = jnp.float32
BF16 = jnp.bfloat16

D_MODEL = 1024
N_META = 16
RMS_EPS = 1e-6
HEADS = 8
QK_NOPE = 128
QK_ROPE = 64
V_HEAD = 128
Q_LORA = 384
KV_LORA = 256
ROPE_BASE = 10000.0
HEAD_PAD = 256
ROPE_LANES = 128
META_PAD = 128
PROJ_COLS = Q_LORA + KV_LORA + ROPE_LANES
LRU_WIDTH = 1024
LRU_BLOCKS = 4
LRU_BLOCK = 256
CONV_WIDTH = 4
LRU_C = 8.0
MASK_VALUE = -1e30
LOG2E = math.log2(math.e)

VMEM_LIMIT_BYTES = 56 * 1024 * 1024

PROJ_ROWS = 512
ATTN_Q = 512
ATTN_K = 512
TAIL_ROWS = 256
CARRY_ROWS = 8


def _rmsnorm(x, g):
    ms = jnp.mean(x * x, axis=-1, keepdims=True)
    return (x * lax.rsqrt(ms + RMS_EPS)) * g


def _rope128(x, c, sa, sb):
    return x * c + pltpu.roll(x, 96, 1) * sa + pltpu.roll(x, 32, 1) * sb


def _silu(x):
    return x * jax.nn.sigmoid(x)


def _dot(a, b):
    return jnp.dot(a, b, preferred_element_type=F32)


def _dot_nt(a, b):
    return lax.dot_general(a, b, (((1,), (1,)), ((), ())), preferred_element_type=F32)


def _mla_project(hn_bf16, w1, qg, kvg, wuq, wukv, c, sa, sb):
    proj = _dot(hn_bf16, w1)
    qln = _rmsnorm(proj[:, :Q_LORA], qg).astype(BF16)
    kvn = _rmsnorm(proj[:, Q_LORA:Q_LORA + KV_LORA], kvg).astype(BF16)
    qf = _dot(qln, wuq)
    parts = []
    for h in range(HEADS):
        parts.append(qf[:, h * HEAD_PAD:h * HEAD_PAD + QK_NOPE])
        parts.append(_rope128(qf[:, h * HEAD_PAD + QK_NOPE:(h + 1) * HEAD_PAD], c, sa, sb))
    q = jnp.concatenate(parts, axis=1).astype(BF16)
    kv = _dot(kvn, wukv)
    k_nope = kv[:, :HEADS * QK_NOPE].astype(BF16)
    v = kv[:, HEADS * QK_NOPE:].astype(BF16)
    k_rope = _rope128(proj[:, Q_LORA + KV_LORA:], c, sa, sb).astype(BF16)
    return q, k_nope, v, k_rope


def _lru_gates(uc, wgate_ref, b_rg, b_ig, lam):
    rs, is_ = [], []
    for g in range(LRU_BLOCKS):
        ri = _dot(uc[:, g * LRU_BLOCK:(g + 1) * LRU_BLOCK].astype(BF16), wgate_ref[g])
        rs.append(ri[:, :LRU_BLOCK])
        is_.append(ri[:, LRU_BLOCK:])
    r = jax.nn.sigmoid(jnp.concatenate(rs, axis=1) + b_rg)
    i = jax.nn.sigmoid(jnp.concatenate(is_, axis=1) + b_ig)
    neg_lam = -lam
    softplus = jnp.maximum(neg_lam, 0.0) + jnp.log1p(jnp.exp(-jnp.abs(neg_lam)))
    log_a = (-LRU_C) * r * softplus
    a = jnp.exp(log_a)
    mult = jnp.sqrt(jnp.tanh(-log_a) * (1.0 + a * a))
    return a, mult, i * uc


def _meta_kernel(meta_ref, ag_ref, w1_ref, wg_ref, qg_ref, kvg_ref, wuq_ref, wukv_ref, wao_ref,
                 bg_ref, wbu_ref, cw_ref, cb_ref, wgate_ref, brg_ref, big_ref, lam_ref,
                 c_ref, sa_ref, sb_ref,
                 kmeta_ref, vmeta_ref, utail_ref, h0_ref):
    x = meta_ref[...]
    hn = _rmsnorm(x, ag_ref[...]).astype(BF16)
    q, k_nope, v, k_rope = _mla_project(hn, w1_ref[...], qg_ref[...], kvg_ref[...], wuq_ref[...],
                                         wukv_ref[...], c_ref[...], sa_ref[...], sb_ref[...])
    gate = _dot(hn, wg_ref[...])
    row = lax.broadcasted_iota(jnp.int32, (N_META, N_META), 0)
    col = lax.broadcasted_iota(jnp.int32, (N_META, N_META), 1)
    kmeta_ref[...] = jnp.zeros(kmeta_ref.shape, kmeta_ref.dtype)
    vmeta_ref[...] = jnp.zeros(vmeta_ref.shape, vmeta_ref.dtype)
    outs = []
    for h in range(HEADS):
        k_h = jnp.concatenate([k_nope[:, h * QK_NOPE:(h + 1) * QK_NOPE], k_rope], axis=1)
        v_h = v[:, h * V_HEAD:(h + 1) * V_HEAD]
        kmeta_ref[h, :N_META, :] = k_h
        vmeta_ref[h, :N_META, :] = v_h
        s = _dot_nt(q[:, h * HEAD_PAD:(h + 1) * HEAD_PAD], k_h)
        s = jnp.where(col <= row, s, MASK_VALUE)
        p = jnp.exp2(s - jnp.max(s, axis=1, keepdims=True))
        p = p / jnp.sum(p, axis=1, keepdims=True)
        outs.append(_dot(p.astype(BF16), v_h))
    attn = jnp.concatenate(outs, axis=1)
    y = (attn * _silu(gate)).astype(BF16)
    h1 = x + _dot(y, wao_ref[...])

    hn1 = _rmsnorm(h1, bg_ref[...]).astype(BF16)
    u = _dot(hn1, wbu_ref[...])
    trow = lax.broadcasted_iota(jnp.int32, (N_META, LRU_WIDTH), 0)
    uc = cb_ref[...] + u * cw_ref[CONV_WIDTH - 1:CONV_WIDTH, :]
    for k in range(1, CONV_WIDTH):
        shifted = jnp.where(trow >= k, pltpu.roll(u, k, 0), 0.0)
        uc = uc + shifted * cw_ref[CONV_WIDTH - 1 - k:CONV_WIDTH - k, :]
    a, mult, iu = _lru_gates(uc, wgate_ref, brg_ref[...], big_ref[...], lam_ref[...])
    b = jnp.where(trow == 0, 1.0, mult) * iu
    h = jnp.zeros((1, LRU_WIDTH), F32)
    for t in range(N_META):
        h = a[t:t + 1, :] * h + b[t:t + 1, :]
    h0_ref[...] = h
    utail_ref[...] = u[N_META - CARRY_ROWS:, :]


def _proj_kernel(x_ref, ag_ref, w1_ref, qg_ref, kvg_ref, wuq_ref, wukv_ref, c_ref, sa_ref, sb_ref,
                 q_ref, kn_ref, v_ref, kr_ref):
    hn = _rmsnorm(x_ref[...], ag_ref[...]).astype(BF16)
    q, k_nope, v, k_rope = _mla_project(hn, w1_ref[...], qg_ref[...], kvg_ref[...], wuq_ref[...],
                                         wukv_ref[...], c_ref[...], sa_ref[...], sb_ref[...])
    q_ref[...] = q
    kn_ref[...] = k_nope
    v_ref[...] = v
    kr_ref[...] = k_rope


def _attn_kernel(q_ref, kn_ref, kr_ref, v_ref, kmeta_ref, vmeta_ref, o_ref, kcat_ref):
    kcat_ref[:, :QK_NOPE] = kn_ref[...]
    kcat_ref[:, QK_NOPE:] = kr_ref[...]
    n_q = q_ref.shape[0] // ATTN_Q
    row = lax.broadcasted_iota(jnp.int32, (ATTN_Q, ATTN_K), 0)
    col = lax.broadcasted_iota(jnp.int32, (ATTN_Q, ATTN_K), 1)
    diag_ok = col <= row
    meta_ok = lax.broadcasted_iota(jnp.int32, (ATTN_Q, META_PAD), 1) < N_META
    for qi in range(n_q):
        q = q_ref[qi * ATTN_Q:(qi + 1) * ATTN_Q, :]
        m = jnp.full((ATTN_Q, 128), MASK_VALUE, F32)
        l = jnp.zeros((ATTN_Q, 128), F32)
        acc = jnp.zeros((ATTN_Q, V_HEAD), F32)
        for j in range(qi + 1):
            last = j == qi
            s = _dot_nt(q, kcat_ref[j * ATTN_K:(j + 1) * ATTN_K, :])
            if last:
                s = jnp.where(diag_ok, s, MASK_VALUE)
                sm = jnp.where(meta_ok, _dot_nt(q, kmeta_ref[...]), MASK_VALUE)
                rowmax = jnp.maximum(jnp.max(s, axis=1, keepdims=True), jnp.max(sm, axis=1, keepdims=True))
            else:
                rowmax = jnp.max(s, axis=1, keepdims=True)
            m_next = jnp.maximum(m, rowmax)
            p = jnp.exp2(s - pltpu.repeat(m_next, ATTN_K // 128, axis=1))
            alpha = jnp.exp2(m - m_next)
            psum = jnp.sum(p, axis=1, keepdims=True)
            pv = _dot(p.astype(BF16), v_ref[j * ATTN_K:(j + 1) * ATTN_K, :])
            if last:
                pm = jnp.exp2(sm - m_next)
                psum = psum + jnp.sum(pm, axis=1, keepdims=True)
                pv = pv + _dot(pm.astype(BF16), vmeta_ref[...])
            l = alpha * l + psum
            acc = alpha * acc + pv
            m = m_next
        o_ref[qi * ATTN_Q:(qi + 1) * ATTN_Q, :] = (acc / l).astype(o_ref.dtype)


def _tail_kernel(x_ref, attn_ref, ag_ref, wg_ref, wao_ref, bg_ref, wbin_ref, cw_ref, cb_ref,
                 wgate_ref, brg_ref, big_ref, lam_ref, wbo_ref, fg_ref, utail_ref, h0_ref,
                 o_ref, ubuf_ref, a_ref, b_ref, hs_ref, hcar_ref):
    ti = pl.program_id(1)

    @pl.when(ti == 0)
    def _():
        ubuf_ref[:CARRY_ROWS, :] = utail_ref[...]
        hcar_ref[...] = h0_ref[...]

    x = x_ref[...]
    hn = _rmsnorm(x, ag_ref[...]).astype(BF16)
    gate = _dot(hn, wg_ref[...])
    y = (attn_ref[...].astype(F32) * _silu(gate)).astype(BF16)
    h1 = x + _dot(y, wao_ref[...])

    hn1 = _rmsnorm(h1, bg_ref[...]).astype(BF16)
    proj = _dot(hn1, wbin_ref[...])
    u = proj[:, :LRU_WIDTH]
    gate_b = proj[:, LRU_WIDTH:]
    ubuf_ref[CARRY_ROWS:, :] = u
    uc = cb_ref[...] + u * cw_ref[CONV_WIDTH - 1:CONV_WIDTH, :]
    for k in range(1, CONV_WIDTH):
        uc = uc + ubuf_ref[CARRY_ROWS - k:CARRY_ROWS - k + TAIL_ROWS, :] * cw_ref[CONV_WIDTH - 1 - k:CONV_WIDTH - k, :]
    ubuf_ref[:CARRY_ROWS, :] = ubuf_ref[TAIL_ROWS:, :]

    a, mult, iu = _lru_gates(uc, wgate_ref, brg_ref[...], big_ref[...], lam_ref[...])
    a_ref[...] = a
    b_ref[...] = mult * iu

    def scan(t, h):
        h = a_ref[pl.ds(t, 1), :] * h + b_ref[pl.ds(t, 1), :]
        hs_ref[pl.ds(t, 1), :] = h
        return h

    hcar_ref[...] = lax.fori_loop(0, TAIL_ROWS, scan, hcar_ref[...], unroll=8)

    y2 = (hs_ref[...] * _silu(gate_b)).astype(BF16)
    h2 = h1 + _dot(y2, wbo_ref[...])
    o_ref[...] = _rmsnorm(h2, fg_ref[...])


def _const_spec(shape):
    zeros = (0,) * len(shape)
    return pl.BlockSpec(shape, lambda *_: zeros, pipeline_mode=pl.Buffered(1))


def _rope_tables(n_pos):
    pos = jnp.arange(n_pos, dtype=F32)
    inv_freq = ROPE_BASE ** (-jnp.arange(0, QK_ROPE, 2, dtype=F32) / QK_ROPE)
    ang = pos[:, None] * inv_freq[None, :]
    cos, sin = jnp.cos(ang), jnp.sin(ang)
    z = jnp.zeros_like(cos)
    c = jnp.concatenate([cos, cos, z, z], axis=1)
    sa = jnp.concatenate([-sin, z, z, z], axis=1)
    sb = jnp.concatenate([z, sin, z, z], axis=1)
    return c, sa, sb


def kernel(x, meta_tokens, a_norm_g, a_w_in, a_q_norm_g, a_kv_norm_g, a_w_uq, a_w_ukv, a_w_out,
           b_norm_g, b_w_in, b_conv_w, b_conv_b, b_w_rg, b_b_rg, b_w_ig, b_b_ig, b_lam, b_w_out,
           final_norm_g):
    B, T, D = x.shape
    assert D == D_MODEL and T % ATTN_Q == 0 and T % PROJ_ROWS == 0 and T % TAIL_ROWS == 0
    assert ATTN_Q == ATTN_K
    assert a_norm_g.shape[0] == 1 and b_norm_g.shape[0] == 1

    w_in = a_w_in[0]
    n_lat = Q_LORA + KV_LORA + QK_ROPE
    w1 = jnp.concatenate([w_in[:, :n_lat], jnp.zeros((D, ROPE_LANES - QK_ROPE), F32)], axis=1).astype(BF16)
    wg = w_in[:, n_lat:].astype(BF16)
    wuq = a_w_uq[0].reshape(Q_LORA, HEADS, QK_NOPE + QK_ROPE) * ((QK_NOPE + QK_ROPE) ** -0.5 * LOG2E)
    wuq = jnp.pad(wuq, ((0, 0), (0, 0), (0, HEAD_PAD - QK_NOPE - QK_ROPE)))
    wuq = wuq.reshape(Q_LORA, HEADS * HEAD_PAD).astype(BF16)
    wukv = a_w_ukv[0].reshape(KV_LORA, HEADS, QK_NOPE + V_HEAD)
    wukv = jnp.concatenate([wukv[:, :, :QK_NOPE].reshape(KV_LORA, HEADS * QK_NOPE),
                            wukv[:, :, QK_NOPE:].reshape(KV_LORA, HEADS * V_HEAD)], axis=1).astype(BF16)
    wao = a_w_out[0].astype(BF16)
    wbin = b_w_in[0].astype(BF16)
    wgate = jnp.concatenate([b_w_rg[0], b_w_ig[0]], axis=2).astype(BF16)
    wbo = b_w_out[0].astype(BF16)
    row = lambda v: v.reshape(1, -1)
    ag, qg, kvg, bg, fg = row(a_norm_g[0]), row(a_q_norm_g[0]), row(a_kv_norm_g[0]), row(b_norm_g[0]), row(final_norm_g)
    cw, cb = b_conv_w[0], row(b_conv_b[0])
    brg, big, lam = row(b_b_rg[0]), row(b_b_ig[0]), row(b_lam[0])
    c_all, sa_all, sb_all = _rope_tables(N_META + T)
    c_m, sa_m, sb_m = c_all[:N_META], sa_all[:N_META], sb_all[:N_META]
    c_r, sa_r, sb_r = c_all[N_META:], sa_all[N_META:], sb_all[N_META:]

    params = functools.partial(pltpu.CompilerParams, vmem_limit_bytes=VMEM_LIMIT_BYTES)

    kmeta, vmeta, utail, h0 = pl.pallas_call(
        _meta_kernel,
        out_shape=(jax.ShapeDtypeStruct((HEADS, META_PAD, HEAD_PAD), BF16),
                   jax.ShapeDtypeStruct((HEADS, META_PAD, V_HEAD), BF16),
                   jax.ShapeDtypeStruct((CARRY_ROWS, LRU_WIDTH), F32),
                   jax.ShapeDtypeStruct((1, LRU_WIDTH), F32)),
        compiler_params=params(),
        name="meta",
    )(meta_tokens, ag, w1, wg, qg, kvg, wuq, wukv, wao, bg, wbin[:, :LRU_WIDTH], cw, cb, wgate,
      brg, big, lam, c_m, sa_m, sb_m)

    n_p = T // PROJ_ROWS
    tile_p = lambda w: pl.BlockSpec((None, PROJ_ROWS, w), lambda b, i: (b, i, 0))
    tab_p = pl.BlockSpec((PROJ_ROWS, ROPE_LANES), lambda b, i: (i, 0))
    cs = _const_spec
    q, k_nope, v, k_rope = pl.pallas_call(
        _proj_kernel,
        grid=(B, n_p),
        in_specs=[tile_p(D), cs((1, D)), cs((D, PROJ_COLS)), cs((1, Q_LORA)), cs((1, KV_LORA)),
                  cs((Q_LORA, HEADS * HEAD_PAD)), cs((KV_LORA, HEADS * (QK_NOPE + V_HEAD))),
                  tab_p, tab_p, tab_p],
        out_specs=(tile_p(HEADS * HEAD_PAD), tile_p(HEADS * QK_NOPE), tile_p(HEADS * V_HEAD), tile_p(ROPE_LANES)),
        out_shape=(jax.ShapeDtypeStruct((B, T, HEADS * HEAD_PAD), BF16),
                   jax.ShapeDtypeStruct((B, T, HEADS * QK_NOPE), BF16),
                   jax.ShapeDtypeStruct((B, T, HEADS * V_HEAD), BF16),
                   jax.ShapeDtypeStruct((B, T, ROPE_LANES), BF16)),
        compiler_params=params(dimension_semantics=("parallel", "parallel")),
        name="proj",
    )(x, ag, w1, qg, kvg, wuq, wukv, c_r, sa_r, sb_r)

    attn = pl.pallas_call(
        _attn_kernel,
        grid=(B, HEADS),
        in_specs=[pl.BlockSpec((None, T, HEAD_PAD), lambda b, h: (b, 0, h)),
                  pl.BlockSpec((None, T, QK_NOPE), lambda b, h: (b, 0, h)),
                  pl.BlockSpec((None, T, ROPE_LANES), lambda b, h: (b, 0, 0)),
                  pl.BlockSpec((None, T, V_HEAD), lambda b, h: (b, 0, h)),
                  pl.BlockSpec((None, META_PAD, HEAD_PAD), lambda b, h: (h, 0, 0)),
                  pl.BlockSpec((None, META_PAD, V_HEAD), lambda b, h: (h, 0, 0))],
        out_specs=pl.BlockSpec((None, T, V_HEAD), lambda b, h: (b, 0, h)),
        out_shape=jax.ShapeDtypeStruct((B, T, HEADS * V_HEAD), BF16),
        scratch_shapes=[pltpu.VMEM((T, HEAD_PAD), BF16)],
        compiler_params=params(dimension_semantics=("parallel", "parallel")),
        name="attn",
    )(q, k_nope, k_rope, v, kmeta, vmeta)

    n_t = T // TAIL_ROWS
    tile_t = pl.BlockSpec((None, TAIL_ROWS, D), lambda b, i: (b, i, 0))
    out = pl.pallas_call(
        _tail_kernel,
        grid=(B, n_t),
        in_specs=[tile_t, tile_t, cs((1, D)), cs((D, D)), cs((D, D)), cs((1, D)), cs((D, 2 * LRU_WIDTH)),
                  cs((CONV_WIDTH, LRU_WIDTH)), cs((1, LRU_WIDTH)),
                  cs((LRU_BLOCKS, LRU_BLOCK, 2 * LRU_BLOCK)), cs((1, LRU_WIDTH)), cs((1, LRU_WIDTH)),
                  cs((1, LRU_WIDTH)), cs((LRU_WIDTH, D)), cs((1, D)),
                  cs((CARRY_ROWS, LRU_WIDTH)), cs((1, LRU_WIDTH))],
        out_specs=tile_t,
        out_shape=jax.ShapeDtypeStruct((B, T, D), x.dtype),
        scratch_shapes=[pltpu.VMEM((CARRY_ROWS + TAIL_ROWS, LRU_WIDTH), F32),
                        pltpu.VMEM((TAIL_ROWS, LRU_WIDTH), F32),
                        pltpu.VMEM((TAIL_ROWS, LRU_WIDTH), F32),
                        pltpu.VMEM((TAIL_ROWS, LRU_WIDTH), F32),
                        pltpu.VMEM((1, LRU_WIDTH), F32)],
        compiler_params=params(dimension_semantics=("arbitrary", "arbitrary")),
        name="tail",
    )(x, attn, ag, wg, wao, bg, wbin, cw, cb, wgate, brg, big, lam, wbo, fg, utail, h0)
    return out
```

```python
import functools
import math

import jax
import jax.numpy as jnp
from jax import lax
from jax.experimental import pallas as pl
from jax.experimental.pallas import tpu as pltpu

F32 = jnp.float32
BF16 = jnp.bfloat16

D_MODEL = 1024
N_META = 16
RMS_EPS = 1e-6
HEADS = 8
QK_NOPE = 128
QK_ROPE = 64
V_HEAD = 128
Q_LORA = 384
KV_LORA = 256
ROPE_BASE = 10000.0
HEAD_PAD = 256
ROPE_LANES = 128
META_PAD = 128
SUM_COLS = 128
PROJ_COLS = Q_LORA + KV_LORA + ROPE_LANES
LRU_WIDTH = 1024
LRU_BLOCKS = 4
LRU_BLOCK = 256
CONV_WIDTH = 4
LRU_C = 8.0
MASK_VALUE = -1e30
LOG2E = math.log2(math.e)

VMEM_LIMIT_BYTES = 56 * 1024 * 1024

PROJ_ROWS = 512
ATTN_Q = 512
ATTN_K = 512
TAIL_ROWS = 512
CARRY_ROWS = 8


def _rmsnorm(x, g):
    ms = jnp.mean(x * x, axis=-1, keepdims=True)
    return (x * lax.rsqrt(ms + RMS_EPS)) * g


def _rope128(x, c, sa, sb):
    return x * c + pltpu.roll(x, 96, 1) * sa + pltpu.roll(x, 32, 1) * sb


def _sigmoid(x):
    return 0.5 * jnp.tanh(0.5 * x) + 0.5


def _silu(x):
    return x * _sigmoid(x)


def _dot(a, b):
    return jnp.dot(a, b, preferred_element_type=F32)


def _dot_nt(a, b):
    return lax.dot_general(a, b, (((1,), (1,)), ((), ())), preferred_element_type=F32)


def _mla_project(hn_bf16, w1, qg, kvg, wuq, wukv, c, sa, sb):
    proj = _dot(hn_bf16, w1)
    qln = _rmsnorm(proj[:, :Q_LORA], qg).astype(BF16)
    kvn = _rmsnorm(proj[:, Q_LORA:Q_LORA + KV_LORA], kvg).astype(BF16)
    qf = _dot(qln, wuq)
    parts = []
    for h in range(HEADS):
        parts.append(qf[:, h * HEAD_PAD:h * HEAD_PAD + QK_NOPE])
        parts.append(_rope128(qf[:, h * HEAD_PAD + QK_NOPE:(h + 1) * HEAD_PAD], c, sa, sb))
    q = jnp.concatenate(parts, axis=1).astype(BF16)
    kv = _dot(kvn, wukv)
    k_nope = kv[:, :HEADS * QK_NOPE].astype(BF16)
    v = kv[:, HEADS * QK_NOPE:].astype(BF16)
    k_rope = _rope128(proj[:, Q_LORA + KV_LORA:], c, sa, sb).astype(BF16)
    return q, k_nope, v, k_rope


def _lru_gates(uc, wgate_ref, b_rg, b_ig, lam):
    rs, is_ = [], []
    for g in range(LRU_BLOCKS):
        ri = _dot(uc[:, g * LRU_BLOCK:(g + 1) * LRU_BLOCK].astype(BF16), wgate_ref[g])
        rs.append(ri[:, :LRU_BLOCK])
        is_.append(ri[:, LRU_BLOCK:])
    r = _sigmoid(jnp.concatenate(rs, axis=1) + b_rg)
    i = _sigmoid(jnp.concatenate(is_, axis=1) + b_ig)
    neg_lam = -lam
    softplus = jnp.maximum(neg_lam, 0.0) + jnp.log1p(jnp.exp(-jnp.abs(neg_lam)))
    log_a = (-LRU_C) * r * softplus
    a = jnp.exp(log_a)
    mult = jnp.sqrt(jnp.tanh(-log_a) * (1.0 + a * a))
    return a, mult, i * uc


def _meta_kernel(meta_ref, ag_ref, w1_ref, wg_ref, qg_ref, kvg_ref, wuq_ref, wukv_ref, wao_ref,
                 bg_ref, wbu_ref, cw_ref, cb_ref, wgate_ref, brg_ref, big_ref, lam_ref,
                 c_ref, sa_ref, sb_ref,
                 kmeta_ref, vmeta_ref, utail_ref, h0_ref):
    x = meta_ref[...]
    hn = _rmsnorm(x, ag_ref[...]).astype(BF16)
    q, k_nope, v, k_rope = _mla_project(hn, w1_ref[...], qg_ref[...], kvg_ref[...], wuq_ref[...],
                                         wukv_ref[...], c_ref[...], sa_ref[...], sb_ref[...])
    gate = _dot(hn, wg_ref[...])
    row = lax.broadcasted_iota(jnp.int32, (N_META, N_META), 0)
    col = lax.broadcasted_iota(jnp.int32, (N_META, N_META), 1)
    kmeta_ref[...] = jnp.zeros(kmeta_ref.shape, kmeta_ref.dtype)
    vmeta_ref[:, :, :V_HEAD] = jnp.zeros((HEADS, META_PAD, V_HEAD), vmeta_ref.dtype)
    vmeta_ref[:, :, V_HEAD:] = jnp.ones((HEADS, META_PAD, SUM_COLS), vmeta_ref.dtype)
    outs = []
    for h in range(HEADS):
        k_h = jnp.concatenate([k_nope[:, h * QK_NOPE:(h + 1) * QK_NOPE], k_rope], axis=1)
        v_h = v[:, h * V_HEAD:(h + 1) * V_HEAD]
        kmeta_ref[h, :N_META, :] = k_h
        vmeta_ref[h, :N_META, :V_HEAD] = v_h
        s = _dot_nt(q[:, h * HEAD_PAD:(h + 1) * HEAD_PAD], k_h)
        s = jnp.where(col <= row, s, MASK_VALUE)
        p = jnp.exp2(s - jnp.max(s, axis=1, keepdims=True))
        p = p / jnp.sum(p, axis=1, keepdims=True)
        outs.append(_dot(p.astype(BF16), v_h))
    attn = jnp.concatenate(outs, axis=1)
    y = (attn * _silu(gate)).astype(BF16)
    h1 = x + _dot(y, wao_ref[...])

    hn1 = _rmsnorm(h1, bg_ref[...]).astype(BF16)
    u = _dot(hn1, wbu_ref[...])
    trow = lax.broadcasted_iota(jnp.int32, (N_META, LRU_WIDTH), 0)
    uc = cb_ref[...] + u * cw_ref[CONV_WIDTH - 1:CONV_WIDTH, :]
    for k in range(1, CONV_WIDTH):
        shifted = jnp.where(trow >= k, pltpu.roll(u, k, 0), 0.0)
        uc = uc + shifted * cw_ref[CONV_WIDTH - 1 - k:CONV_WIDTH - k, :]
    a, mult, iu = _lru_gates(uc, wgate_ref, brg_ref[...], big_ref[...], lam_ref[...])
    b = jnp.where(trow == 0, 1.0, mult) * iu
    h = jnp.zeros((1, LRU_WIDTH), F32)
    for t in range(N_META):
        h = a[t:t + 1, :] * h + b[t:t + 1, :]
    h0_ref[...] = h
    utail_ref[...] = u[N_META - CARRY_ROWS:, :]


def _proj_kernel(x_ref, ag_ref, w1_ref, qg_ref, kvg_ref, wuq_ref, wukv_ref, c_ref, sa_ref, sb_ref,
                 q_ref, kn_ref, v_ref, kr_ref):
    hn = _rmsnorm(x_ref[...], ag_ref[...]).astype(BF16)
    q, k_nope, v, k_rope = _mla_project(hn, w1_ref[...], qg_ref[...], kvg_ref[...], wuq_ref[...],
                                         wukv_ref[...], c_ref[...], sa_ref[...], sb_ref[...])
    q_ref[...] = q
    kn_ref[...] = k_nope
    v_ref[...] = v
    kr_ref[...] = k_rope


def _attn_kernel(q_ref, kn_ref, kr_ref, v_ref, kmeta_ref, vmeta_ref, o_ref, kcat_ref, vext_ref):
    kcat_ref[:, :QK_NOPE] = kn_ref[...]
    kcat_ref[:, QK_NOPE:] = kr_ref[...]
    vext_ref[:, :V_HEAD] = v_ref[...]
    vext_ref[:, V_HEAD:] = jnp.ones((vext_ref.shape[0], SUM_COLS), vext_ref.dtype)
    n_q = q_ref.shape[0] // ATTN_Q
    row = lax.broadcasted_iota(jnp.int32, (ATTN_Q, ATTN_K), 0)
    col = lax.broadcasted_iota(jnp.int32, (ATTN_Q, ATTN_K), 1)
    diag_ok = col <= row
    meta_ok = lax.broadcasted_iota(jnp.int32, (ATTN_Q, META_PAD), 1) < N_META
    for qi in range(n_q):
        q = q_ref[qi * ATTN_Q:(qi + 1) * ATTN_Q, :]
        m = jnp.full((ATTN_Q, 128), MASK_VALUE, F32)
        acc = jnp.zeros((ATTN_Q, V_HEAD + SUM_COLS), F32)
        for j in range(qi + 1):
            last = j == qi
            s = _dot_nt(q, kcat_ref[j * ATTN_K:(j + 1) * ATTN_K, :])
            if last:
                s = jnp.where(diag_ok, s, MASK_VALUE)
                sm = jnp.where(meta_ok, _dot_nt(q, kmeta_ref[...]), MASK_VALUE)
                rowmax = jnp.maximum(jnp.max(s, axis=1, keepdims=True), jnp.max(sm, axis=1, keepdims=True))
            else:
                rowmax = jnp.max(s, axis=1, keepdims=True)
            m_next = jnp.maximum(m, rowmax)
            p = jnp.exp2(s - pltpu.repeat(m_next, ATTN_K // 128, axis=1)).astype(BF16)
            alpha = jnp.exp2(m - m_next)
            pv = _dot(p, vext_ref[j * ATTN_K:(j + 1) * ATTN_K, :])
            if last:
                pv = pv + _dot(jnp.exp2(sm - m_next).astype(BF16), vmeta_ref[...])
            acc = pltpu.repeat(alpha, 2, axis=1) * acc + pv
            m = m_next
        out = acc[:, :V_HEAD] / acc[:, V_HEAD:]
        o_ref[qi * ATTN_Q:(qi + 1) * ATTN_Q, :] = out.astype(o_ref.dtype)


def _tail_kernel(x_ref, attn_ref, ag_ref, wg_ref, wao_ref, bg_ref, wbin_ref, cw_ref, cb_ref,
                 wgate_ref, brg_ref, big_ref, lam_ref, wbo_ref, fg_ref, utail_ref, h0_ref,
                 o_ref, ubuf_ref, a_ref, b_ref, hs_ref, hcar_ref):
    ti = pl.program_id(1)

    @pl.when(ti == 0)
    def _():
        ubuf_ref[:CARRY_ROWS, :] = utail_ref[...]
        hcar_ref[...] = h0_ref[...]

    x = x_ref[...]
    hn = _rmsnorm(x, ag_ref[...]).astype(BF16)
    gate = _dot(hn, wg_ref[...])
    y = (attn_ref[...].astype(F32) * _silu(gate)).astype(BF16)
    h1 = x + _dot(y, wao_ref[...])

    hn1 = _rmsnorm(h1, bg_ref[...]).astype(BF16)
    proj = _dot(hn1, wbin_ref[...])
    u = proj[:, :LRU_WIDTH]
    gate_b = proj[:, LRU_WIDTH:]
    ubuf_ref[CARRY_ROWS:, :] = u
    uc = cb_ref[...] + u * cw_ref[CONV_WIDTH - 1:CONV_WIDTH, :]
    for k in range(1, CONV_WIDTH):
        uc = uc + ubuf_ref[CARRY_ROWS - k:CARRY_ROWS - k + TAIL_ROWS, :] * cw_ref[CONV_WIDTH - 1 - k:CONV_WIDTH - k, :]
    ubuf_ref[:CARRY_ROWS, :] = ubuf_ref[TAIL_ROWS:, :]

    a, mult, iu = _lru_gates(uc, wgate_ref, brg_ref[...], big_ref[...], lam_ref[...])
    a_ref[...] = a
    b_ref[...] = mult * iu

    def scan(t, h):
        h = a_ref[pl.ds(t, 1), :] * h + b_ref[pl.ds(t, 1), :]
        hs_ref[pl.ds(t, 1), :] = h
        return h

    hcar_ref[...] = lax.fori_loop(0, TAIL_ROWS, scan, hcar_ref[...], unroll=8)

    y2 = (hs_ref[...] * _silu(gate_b)).astype(BF16)
    h2 = h1 + _dot(y2, wbo_ref[...])
    o_ref[...] = _rmsnorm(h2, fg_ref[...])


def _const_spec(shape):
    zeros = (0,) * len(shape)
    return pl.BlockSpec(shape, lambda *_: zeros, pipeline_mode=pl.Buffered(1))


def _rope_tables(n_pos):
    pos = jnp.arange(n_pos, dtype=F32)
    inv_freq = ROPE_BASE ** (-jnp.arange(0, QK_ROPE, 2, dtype=F32) / QK_ROPE)
    ang = pos[:, None] * inv_freq[None, :]
    cos, sin = jnp.cos(ang), jnp.sin(ang)
    z = jnp.zeros_like(cos)
    c = jnp.concatenate([cos, cos, z, z], axis=1)
    sa = jnp.concatenate([-sin, z, z, z], axis=1)
    sb = jnp.concatenate([z, sin, z, z], axis=1)
    return c, sa, sb


def kernel(x, meta_tokens, a_norm_g, a_w_in, a_q_norm_g, a_kv_norm_g, a_w_uq, a_w_ukv, a_w_out,
           b_norm_g, b_w_in, b_conv_w, b_conv_b, b_w_rg, b_b_rg, b_w_ig, b_b_ig, b_lam, b_w_out,
           final_norm_g):
    B, T, D = x.shape
    assert D == D_MODEL and T % ATTN_Q == 0 and T % PROJ_ROWS == 0 and T % TAIL_ROWS == 0
    assert ATTN_Q == ATTN_K
    assert a_norm_g.shape[0] == 1 and b_norm_g.shape[0] == 1

    w_in = a_w_in[0]
    n_lat = Q_LORA + KV_LORA + QK_ROPE
    w1 = jnp.concatenate([w_in[:, :n_lat], jnp.zeros((D, ROPE_LANES - QK_ROPE), F32)], axis=1).astype(BF16)
    wg = w_in[:, n_lat:].astype(BF16)
    wuq = a_w_uq[0].reshape(Q_LORA, HEADS, QK_NOPE + QK_ROPE) * ((QK_NOPE + QK_ROPE) ** -0.5 * LOG2E)
    wuq = jnp.pad(wuq, ((0, 0), (0, 0), (0, HEAD_PAD - QK_NOPE - QK_ROPE)))
    wuq = wuq.reshape(Q_LORA, HEADS * HEAD_PAD).astype(BF16)
    wukv = a_w_ukv[0].reshape(KV_LORA, HEADS, QK_NOPE + V_HEAD)
    wukv = jnp.concatenate([wukv[:, :, :QK_NOPE].reshape(KV_LORA, HEADS * QK_NOPE),
                            wukv[:, :, QK_NOPE:].reshape(KV_LORA, HEADS * V_HEAD)], axis=1).astype(BF16)
    wao = a_w_out[0].astype(BF16)
    wbin = b_w_in[0].astype(BF16)
    wgate = jnp.concatenate([b_w_rg[0], b_w_ig[0]], axis=2).astype(BF16)
    wbo = b_w_out[0].astype(BF16)
    row = lambda v: v.reshape(1, -1)
    ag, qg, kvg, bg, fg = row(a_norm_g[0]), row(a_q_norm_g[0]), row(a_kv_norm_g[0]), row(b_norm_g[0]), row(final_norm_g)
    cw, cb = b_conv_w[0], row(b_conv_b[0])
    brg, big, lam = row(b_b_rg[0]), row(b_b_ig[0]), row(b_lam[0])
    c_all, sa_all, sb_all = _rope_tables(N_META + T)
    c_m, sa_m, sb_m = c_all[:N_META], sa_all[:N_META], sb_all[:N_META]
    c_r, sa_r, sb_r = c_all[N_META:], sa_all[N_META:], sb_all[N_META:]

    params = functools.partial(pltpu.CompilerParams, vmem_limit_bytes=VMEM_LIMIT_BYTES)

    kmeta, vmeta, utail, h0 = pl.pallas_call(
        _meta_kernel,
        out_shape=(jax.ShapeDtypeStruct((HEADS, META_PAD, HEAD_PAD), BF16),
                   jax.ShapeDtypeStruct((HEADS, META_PAD, V_HEAD + SUM_COLS), BF16),
                   jax.ShapeDtypeStruct((CARRY_ROWS, LRU_WIDTH), F32),
                   jax.ShapeDtypeStruct((1, LRU_WIDTH), F32)),
        compiler_params=params(),
        name="meta",
    )(meta_tokens, ag, w1, wg, qg, kvg, wuq, wukv, wao, bg, wbin[:, :LRU_WIDTH], cw, cb, wgate,
      brg, big, lam, c_m, sa_m, sb_m)

    n_p = T // PROJ_ROWS
    tile_p = lambda w: pl.BlockSpec((None, PROJ_ROWS, w), lambda b, i: (b, i, 0))
    tab_p = pl.BlockSpec((PROJ_ROWS, ROPE_LANES), lambda b, i: (i, 0))
    cs = _const_spec
    q, k_nope, v, k_rope = pl.pallas_call(
        _proj_kernel,
        grid=(B, n_p),
        in_specs=[tile_p(D), cs((1, D)), cs((D, PROJ_COLS)), cs((1, Q_LORA)), cs((1, KV_LORA)),
                  cs((Q_LORA, HEADS * HEAD_PAD)), cs((KV_LORA, HEADS * (QK_NOPE + V_HEAD))),
                  tab_p, tab_p, tab_p],
        out_specs=(tile_p(HEADS * HEAD_PAD), tile_p(HEADS * QK_NOPE), tile_p(HEADS * V_HEAD), tile_p(ROPE_LANES)),
        out_shape=(jax.ShapeDtypeStruct((B, T, HEADS * HEAD_PAD), BF16),
                   jax.ShapeDtypeStruct((B, T, HEADS * QK_NOPE), BF16),
                   jax.ShapeDtypeStruct((B, T, HEADS * V_HEAD), BF16),
                   jax.ShapeDtypeStruct((B, T, ROPE_LANES), BF16)),
        compiler_params=params(dimension_semantics=("parallel", "parallel")),
        name="proj",
    )(x, ag, w1, qg, kvg, wuq, wukv, c_r, sa_r, sb_r)

    attn = pl.pallas_call(
        _attn_kernel,
        grid=(B, HEADS),
        in_specs=[pl.BlockSpec((None, T, HEAD_PAD), lambda b, h: (b, 0, h)),
                  pl.BlockSpec((None, T, QK_NOPE), lambda b, h: (b, 0, h)),
                  pl.BlockSpec((None, T, ROPE_LANES), lambda b, h: (b, 0, 0)),
                  pl.BlockSpec((None, T, V_HEAD), lambda b, h: (b, 0, h)),
                  pl.BlockSpec((None, META_PAD, HEAD_PAD), lambda b, h: (h, 0, 0)),
                  pl.BlockSpec((None, META_PAD, V_HEAD + SUM_COLS), lambda b, h: (h, 0, 0))],
        out_specs=pl.BlockSpec((None, T, V_HEAD), lambda b, h: (b, 0, h)),
        out_shape=jax.ShapeDtypeStruct((B, T, HEADS * V_HEAD), BF16),
        scratch_shapes=[pltpu.VMEM((T, HEAD_PAD), BF16),
                        pltpu.VMEM((T, V_HEAD + SUM_COLS), BF16)],
        compiler_params=params(dimension_semantics=("parallel", "parallel")),
        name="attn",
    )(q, k_nope, k_rope, v, kmeta, vmeta)

    n_t = T // TAIL_ROWS
    tile_t = pl.BlockSpec((None, TAIL_ROWS, D), lambda b, i: (b, i, 0))
    out = pl.pallas_call(
        _tail_kernel,
        grid=(B, n_t),
        in_specs=[tile_t, tile_t, cs((1, D)), cs((D, D)), cs((D, D)), cs((1, D)), cs((D, 2 * LRU_WIDTH)),
                  cs((CONV_WIDTH, LRU_WIDTH)), cs((1, LRU_WIDTH)),
                  cs((LRU_BLOCKS, LRU_BLOCK, 2 * LRU_BLOCK)), cs((1, LRU_WIDTH)), cs((1, LRU_WIDTH)),
                  cs((1, LRU_WIDTH)), cs((LRU_WIDTH, D)), cs((1, D)),
                  cs((CARRY_ROWS, LRU_WIDTH)), cs((1, LRU_WIDTH))],
        out_specs=tile_t,
        out_shape=jax.ShapeDtypeStruct((B, T, D), x.dtype),
        scratch_shapes=[pltpu.VMEM((CARRY_ROWS + TAIL_ROWS, LRU_WIDTH), F32),
                        pltpu.VMEM((TAIL_ROWS, LRU_WIDTH), F32),
                        pltpu.VMEM((TAIL_ROWS, LRU_WIDTH), F32),
                        pltpu.VMEM((TAIL_ROWS, LRU_WIDTH), F32),
                        pltpu.VMEM((1, LRU_WIDTH), F32)],
        compiler_params=params(dimension_semantics=("arbitrary", "arbitrary")),
        name="tail",
    )(x, attn, ag, wg, wao, bg, wbin, cw, cb, wgate, brg, big, lam, wbo, fg, utail, h0)
    return out
```

```python
import functools
import math

import jax
import jax.numpy as jnp
from jax import lax
from jax.experimental import pallas as pl
from jax.experimental.pallas import tpu as pltpu

F32 = jnp.float32
BF16 = jnp.bfloat16

D_MODEL = 1024
N_META = 16
RMS_EPS = 1e-6
HEADS = 8
QK_NOPE = 128
QK_ROPE = 64
V_HEAD = 128
Q_LORA = 384
KV_LORA = 256
ROPE_BASE = 10000.0
HEAD_PAD = 256
ROPE_LANES = 128
META_PAD = 128
SUM_COLS = 128
PROJ_COLS = Q_LORA + KV_LORA + ROPE_LANES
LRU_WIDTH = 1024
LRU_BLOCKS = 4
LRU_BLOCK = 256
CONV_WIDTH = 4
LRU_C = 8.0
MASK_VALUE = -1e30
LOG2E = math.log2(math.e)

VMEM_LIMIT_BYTES = 56 * 1024 * 1024

PROJ_ROWS = 512
ATTN_Q = 512
ATTN_K = 512
TAIL_T = 64
CARRY_ROWS = 8
LANES = 128
SUBLANES = 8
N_SLAB = LRU_WIDTH // LANES


def _rmsnorm(x, g):
    ms = jnp.mean(x * x, axis=-1, keepdims=True)
    return (x * lax.rsqrt(ms + RMS_EPS)) * g


def _rope128(x, c, sa, sb):
    return x * c + pltpu.roll(x, 96, 1) * sa + pltpu.roll(x, 32, 1) * sb


def _sigmoid(x):
    return 0.5 * jnp.tanh(0.5 * x) + 0.5


def _silu(x):
    return x * _sigmoid(x)


def _dot(a, b):
    return jnp.dot(a, b, preferred_element_type=F32)


def _dot_nt(a, b):
    return lax.dot_general(a, b, (((1,), (1,)), ((), ())), preferred_element_type=F32)


def _mla_project(hn_bf16, w1, qg, kvg, wuq, wukv, c, sa, sb):
    proj = _dot(hn_bf16, w1)
    qln = _rmsnorm(proj[:, :Q_LORA], qg).astype(BF16)
    kvn = _rmsnorm(proj[:, Q_LORA:Q_LORA + KV_LORA], kvg).astype(BF16)
    qf = _dot(qln, wuq)
    parts = []
    for h in range(HEADS):
        parts.append(qf[:, h * HEAD_PAD:h * HEAD_PAD + QK_NOPE])
        parts.append(_rope128(qf[:, h * HEAD_PAD + QK_NOPE:(h + 1) * HEAD_PAD], c, sa, sb))
    q = jnp.concatenate(parts, axis=1).astype(BF16)
    kv = _dot(kvn, wukv)
    k_nope = kv[:, :HEADS * QK_NOPE].astype(BF16)
    v = kv[:, HEADS * QK_NOPE:].astype(BF16)
    k_rope = _rope128(proj[:, Q_LORA + KV_LORA:], c, sa, sb).astype(BF16)
    return q, k_nope, v, k_rope


def _lru_terms(r_lin, i_lin, uc, b_rg, b_ig, lam):
    r = _sigmoid(r_lin + b_rg)
    i = _sigmoid(i_lin + b_ig)
    neg_lam = -lam
    softplus = jnp.maximum(neg_lam, 0.0) + jnp.log1p(jnp.exp(-jnp.abs(neg_lam)))
    log_a = (-LRU_C) * r * softplus
    a = jnp.exp(log_a)
    mult = jnp.sqrt(jnp.tanh(-log_a) * (1.0 + a * a))
    return a, mult, i * uc


def _lru_gates(uc, wgate_ref, b_rg, b_ig, lam):
    rs, is_ = [], []
    for g in range(LRU_BLOCKS):
        ri = _dot(uc[:, g * LRU_BLOCK:(g + 1) * LRU_BLOCK].astype(BF16), wgate_ref[g])
        rs.append(ri[:, :LRU_BLOCK])
        is_.append(ri[:, LRU_BLOCK:])
    return _lru_terms(jnp.concatenate(rs, axis=1), jnp.concatenate(is_, axis=1), uc, b_rg, b_ig, lam)


def _meta_kernel(meta_ref, ag_ref, w1_ref, wg_ref, qg_ref, kvg_ref, wuq_ref, wukv_ref, wao_ref,
                 bg_ref, wbu_ref, cw_ref, cb_ref, wgate_ref, brg_ref, big_ref, lam_ref,
                 c_ref, sa_ref, sb_ref,
                 kmeta_ref, vmeta_ref, utail_ref, h0_ref):
    x = meta_ref[...]
    hn = _rmsnorm(x, ag_ref[...]).astype(BF16)
    q, k_nope, v, k_rope = _mla_project(hn, w1_ref[...], qg_ref[...], kvg_ref[...], wuq_ref[...],
                                         wukv_ref[...], c_ref[...], sa_ref[...], sb_ref[...])
    gate = _dot(hn, wg_ref[...])
    row = lax.broadcasted_iota(jnp.int32, (N_META, N_META), 0)
    col = lax.broadcasted_iota(jnp.int32, (N_META, N_META), 1)
    kmeta_ref[...] = jnp.zeros(kmeta_ref.shape, kmeta_ref.dtype)
    vmeta_ref[:, :, :V_HEAD] = jnp.zeros((HEADS, META_PAD, V_HEAD), vmeta_ref.dtype)
    vmeta_ref[:, :, V_HEAD:] = jnp.ones((HEADS, META_PAD, SUM_COLS), vmeta_ref.dtype)
    outs = []
    for h in range(HEADS):
        k_h = jnp.concatenate([k_nope[:, h * QK_NOPE:(h + 1) * QK_NOPE], k_rope], axis=1)
        v_h = v[:, h * V_HEAD:(h + 1) * V_HEAD]
        kmeta_ref[h, :N_META, :] = k_h
        vmeta_ref[h, :N_META, :V_HEAD] = v_h
        s = _dot_nt(q[:, h * HEAD_PAD:(h + 1) * HEAD_PAD], k_h)
        s = jnp.where(col <= row, s, MASK_VALUE)
        p = jnp.exp2(s - jnp.max(s, axis=1, keepdims=True))
        p = p / jnp.sum(p, axis=1, keepdims=True)
        outs.append(_dot(p.astype(BF16), v_h))
    attn = jnp.concatenate(outs, axis=1)
    y = (attn * _silu(gate)).astype(BF16)
    h1 = x + _dot(y, wao_ref[...])

    hn1 = _rmsnorm(h1, bg_ref[...]).astype(BF16)
    u = _dot(hn1, wbu_ref[...])
    trow = lax.broadcasted_iota(jnp.int32, (N_META, LRU_WIDTH), 0)
    uc = cb_ref[...] + u * cw_ref[CONV_WIDTH - 1:CONV_WIDTH, :]
    for k in range(1, CONV_WIDTH):
        shifted = jnp.where(trow >= k, pltpu.roll(u, k, 0), 0.0)
        uc = uc + shifted * cw_ref[CONV_WIDTH - 1 - k:CONV_WIDTH - k, :]
    a, mult, iu = _lru_gates(uc, wgate_ref, brg_ref[...], big_ref[...], lam_ref[...])
    b = jnp.where(trow == 0, 1.0, mult) * iu
    h = jnp.zeros((1, LRU_WIDTH), F32)
    for t in range(N_META):
        h = a[t:t + 1, :] * h + b[t:t + 1, :]
    h0_ref[...] = h
    utail_ref[...] = u[N_META - CARRY_ROWS:, :]


def _proj_kernel(x_ref, ag_ref, w1_ref, qg_ref, kvg_ref, wuq_ref, wukv_ref, c_ref, sa_ref, sb_ref,
                 q_ref, kn_ref, v_ref, kr_ref):
    hn = _rmsnorm(x_ref[...], ag_ref[...]).astype(BF16)
    q, k_nope, v, k_rope = _mla_project(hn, w1_ref[...], qg_ref[...], kvg_ref[...], wuq_ref[...],
                                         wukv_ref[...], c_ref[...], sa_ref[...], sb_ref[...])
    q_ref[...] = q
    kn_ref[...] = k_nope
    v_ref[...] = v
    kr_ref[...] = k_rope


def _attn_kernel(q_ref, kn_ref, kr_ref, v_ref, kmeta_ref, vmeta_ref, o_ref, kcat_ref, vext_ref):
    kcat_ref[:, :QK_NOPE] = kn_ref[...]
    kcat_ref[:, QK_NOPE:] = kr_ref[...]
    vext_ref[:, :V_HEAD] = v_ref[...]
    vext_ref[:, V_HEAD:] = jnp.ones((vext_ref.shape[0], SUM_COLS), vext_ref.dtype)
    n_q = q_ref.shape[0] // ATTN_Q
    row = lax.broadcasted_iota(jnp.int32, (ATTN_Q, ATTN_K), 0)
    col = lax.broadcasted_iota(jnp.int32, (ATTN_Q, ATTN_K), 1)
    diag_ok = col <= row
    meta_ok = lax.broadcasted_iota(jnp.int32, (ATTN_Q, META_PAD), 1) < N_META
    for qi in range(n_q):
        q = q_ref[qi * ATTN_Q:(qi + 1) * ATTN_Q, :]
        m = jnp.full((ATTN_Q, 128), MASK_VALUE, F32)
        acc = jnp.zeros((ATTN_Q, V_HEAD + SUM_COLS), F32)
        for j in range(qi + 1):
            last = j == qi
            s = _dot_nt(q, kcat_ref[j * ATTN_K:(j + 1) * ATTN_K, :])
            if last:
                s = jnp.where(diag_ok, s, MASK_VALUE)
                sm = jnp.where(meta_ok, _dot_nt(q, kmeta_ref[...]), MASK_VALUE)
                rowmax = jnp.maximum(jnp.max(s, axis=1, keepdims=True), jnp.max(sm, axis=1, keepdims=True))
            else:
                rowmax = jnp.max(s, axis=1, keepdims=True)
            m_next = jnp.maximum(m, rowmax)
            p = jnp.exp2(s - pltpu.repeat(m_next, ATTN_K // 128, axis=1)).astype(BF16)
            alpha = jnp.exp2(m - m_next)
            pv = _dot(p, vext_ref[j * ATTN_K:(j + 1) * ATTN_K, :])
            if last:
                pv = pv + _dot(jnp.exp2(sm - m_next).astype(BF16), vmeta_ref[...])
            acc = pltpu.repeat(alpha, 2, axis=1) * acc + pv
            m = m_next
        out = acc[:, :V_HEAD] / acc[:, V_HEAD:]
        o_ref[qi * ATTN_Q:(qi + 1) * ATTN_Q, :] = out.astype(o_ref.dtype)


def _tail_kernel(x_ref, attn_ref, ag_ref, wg_ref, wao_ref, bg_ref, wbin_ref, cw_ref, cb_ref,
                 wgate_ref, brg_ref, big_ref, lam_ref, wbo_ref, fg_ref, utail_ref, h0_ref,
                 o_ref, uext_ref, a_ref, b_ref, hs_ref, hcar_ref):
    ti = pl.program_id(0)
    nb, tt, d = x_ref.shape
    rows = nb * tt
    carry = (CONV_WIDTH - 1) * nb

    @pl.when(ti == 0)
    def _():
        for g in range(N_SLAB):
            lanes = slice(g * LANES, (g + 1) * LANES)
            for j in range(CONV_WIDTH - 1):
                src = CARRY_ROWS - (CONV_WIDTH - 1) + j
                uext_ref[g, j * nb:(j + 1) * nb, :] = jnp.broadcast_to(utail_ref[src:src + 1, lanes], (nb, LANES))
            hcar_ref[g] = jnp.broadcast_to(h0_ref[:, lanes], (nb, LANES))

    x = x_ref[...].reshape(rows, d)
    hn = _rmsnorm(x, ag_ref[...]).astype(BF16)
    gate = _dot(hn, wg_ref[...])
    y = (attn_ref[...].reshape(rows, d).astype(F32) * _silu(gate)).astype(BF16)
    h1 = x + _dot(y, wao_ref[...])

    hn1 = _rmsnorm(h1, bg_ref[...]).astype(BF16)
    proj = _dot(hn1, wbin_ref[...])
    u = proj[:, :LRU_WIDTH]
    gate_b = proj[:, LRU_WIDTH:]
    for g in range(N_SLAB):
        for b in range(nb):
            uext_ref[g, pl.ds(carry + b, tt, stride=nb), :] = u[b * tt:(b + 1) * tt, g * LANES:(g + 1) * LANES]

    for blk in range(LRU_BLOCKS):
        ucs = []
        for half in range(LRU_BLOCK // LANES):
            g = blk * (LRU_BLOCK // LANES) + half
            lanes = slice(g * LANES, (g + 1) * LANES)
            uc = cb_ref[:, lanes]
            for k in range(CONV_WIDTH):
                uc = uc + uext_ref[g, carry - k * nb:carry - k * nb + rows, :] * cw_ref[CONV_WIDTH - 1 - k:CONV_WIDTH - k, lanes]
            ucs.append(uc)
        ri = _dot(jnp.concatenate(ucs, axis=1).astype(BF16), wgate_ref[blk])
        for half in range(LRU_BLOCK // LANES):
            g = blk * (LRU_BLOCK // LANES) + half
            lanes = slice(g * LANES, (g + 1) * LANES)
            a, mult, iu = _lru_terms(ri[:, half * LANES:(half + 1) * LANES],
                                     ri[:, LRU_BLOCK + half * LANES:LRU_BLOCK + (half + 1) * LANES],
                                     ucs[half], brg_ref[:, lanes], big_ref[:, lanes], lam_ref[:, lanes])
            a_ref[g] = a
            b_ref[g] = mult * iu
    for g in range(N_SLAB):
        uext_ref[g, :carry, :] = uext_ref[g, rows:rows + carry, :]

    def scan(t, hs):
        r = pl.multiple_of(t * nb, nb)
        new = []
        for g in range(N_SLAB):
            h = a_ref[g, pl.ds(r, nb), :] * hs[g] + b_ref[g, pl.ds(r, nb), :]
            hs_ref[g, pl.ds(r, nb), :] = h
            new.append(h)
        return tuple(new)

    h_last = lax.fori_loop(0, tt, scan, tuple(hcar_ref[g] for g in range(N_SLAB)), unroll=8)
    for g in range(N_SLAB):
        hcar_ref[g] = h_last[g]

    hs = jnp.concatenate(
        [jnp.concatenate([hs_ref[g, pl.ds(b, tt, stride=nb), :] for b in range(nb)], axis=0)
         for g in range(N_SLAB)], axis=1)
    y2 = (hs * _silu(gate_b)).astype(BF16)
    h2 = h1 + _dot(y2, wbo_ref[...])
    o_ref[...] = _rmsnorm(h2, fg_ref[...]).reshape(nb, tt, d)


def _const_spec(shape):
    zeros = (0,) * len(shape)
    return pl.BlockSpec(shape, lambda *_: zeros, pipeline_mode=pl.Buffered(1))


def _rope_tables(n_pos):
    pos = jnp.arange(n_pos, dtype=F32)
    inv_freq = ROPE_BASE ** (-jnp.arange(0, QK_ROPE, 2, dtype=F32) / QK_ROPE)
    ang = pos[:, None] * inv_freq[None, :]
    cos, sin = jnp.cos(ang), jnp.sin(ang)
    z = jnp.zeros_like(cos)
    c = jnp.concatenate([cos, cos, z, z], axis=1)
    sa = jnp.concatenate([-sin, z, z, z], axis=1)
    sb = jnp.concatenate([z, sin, z, z], axis=1)
    return c, sa, sb


def kernel(x, meta_tokens, a_norm_g, a_w_in, a_q_norm_g, a_kv_norm_g, a_w_uq, a_w_ukv, a_w_out,
           b_norm_g, b_w_in, b_conv_w, b_conv_b, b_w_rg, b_b_rg, b_w_ig, b_b_ig, b_lam, b_w_out,
           final_norm_g):
    B, T, D = x.shape
    assert D == D_MODEL and T % ATTN_Q == 0 and T % PROJ_ROWS == 0 and T % TAIL_T == 0
    assert B == SUBLANES
    assert ATTN_Q == ATTN_K
    assert a_norm_g.shape[0] == 1 and b_norm_g.shape[0] == 1

    w_in = a_w_in[0]
    n_lat = Q_LORA + KV_LORA + QK_ROPE
    w1 = jnp.concatenate([w_in[:, :n_lat], jnp.zeros((D, ROPE_LANES - QK_ROPE), F32)], axis=1).astype(BF16)
    wg = w_in[:, n_lat:].astype(BF16)
    wuq = a_w_uq[0].reshape(Q_LORA, HEADS, QK_NOPE + QK_ROPE) * ((QK_NOPE + QK_ROPE) ** -0.5 * LOG2E)
    wuq = jnp.pad(wuq, ((0, 0), (0, 0), (0, HEAD_PAD - QK_NOPE - QK_ROPE)))
    wuq = wuq.reshape(Q_LORA, HEADS * HEAD_PAD).astype(BF16)
    wukv = a_w_ukv[0].reshape(KV_LORA, HEADS, QK_NOPE + V_HEAD)
    wukv = jnp.concatenate([wukv[:, :, :QK_NOPE].reshape(KV_LORA, HEADS * QK_NOPE),
                            wukv[:, :, QK_NOPE:].reshape(KV_LORA, HEADS * V_HEAD)], axis=1).astype(BF16)
    wao = a_w_out[0].astype(BF16)
    wbin = b_w_in[0].astype(BF16)
    wgate = jnp.concatenate([b_w_rg[0], b_w_ig[0]], axis=2).astype(BF16)
    wbo = b_w_out[0].astype(BF16)
    row = lambda v: v.reshape(1, -1)
    ag, qg, kvg, bg, fg = row(a_norm_g[0]), row(a_q_norm_g[0]), row(a_kv_norm_g[0]), row(b_norm_g[0]), row(final_norm_g)
    cw, cb = b_conv_w[0], row(b_conv_b[0])
    brg, big, lam = row(b_b_rg[0]), row(b_b_ig[0]), row(b_lam[0])
    c_all, sa_all, sb_all = _rope_tables(N_META + T)
    c_m, sa_m, sb_m = c_all[:N_META], sa_all[:N_META], sb_all[:N_META]
    c_r, sa_r, sb_r = c_all[N_META:], sa_all[N_META:], sb_all[N_META:]

    params = functools.partial(pltpu.CompilerParams, vmem_limit_bytes=VMEM_LIMIT_BYTES)

    kmeta, vmeta, utail, h0 = pl.pallas_call(
        _meta_kernel,
        out_shape=(jax.ShapeDtypeStruct((HEADS, META_PAD, HEAD_PAD), BF16),
                   jax.ShapeDtypeStruct((HEADS, META_PAD, V_HEAD + SUM_COLS), BF16),
                   jax.ShapeDtypeStruct((CARRY_ROWS, LRU_WIDTH), F32),
                   jax.ShapeDtypeStruct((1, LRU_WIDTH), F32)),
        compiler_params=params(),
        name="meta",
    )(meta_tokens, ag, w1, wg, qg, kvg, wuq, wukv, wao, bg, wbin[:, :LRU_WIDTH], cw, cb, wgate,
      brg, big, lam, c_m, sa_m, sb_m)

    n_p = T // PROJ_ROWS
    tile_p = lambda w: pl.BlockSpec((None, PROJ_ROWS, w), lambda b, i: (b, i, 0))
    tab_p = pl.BlockSpec((PROJ_ROWS, ROPE_LANES), lambda b, i: (i, 0))
    cs = _const_spec
    q, k_nope, v, k_rope = pl.pallas_call(
        _proj_kernel,
        grid=(B, n_p),
        in_specs=[tile_p(D), cs((1, D)), cs((D, PROJ_COLS)), cs((1, Q_LORA)), cs((1, KV_LORA)),
                  cs((Q_LORA, HEADS * HEAD_PAD)), cs((KV_LORA, HEADS * (QK_NOPE + V_HEAD))),
                  tab_p, tab_p, tab_p],
        out_specs=(tile_p(HEADS * HEAD_PAD), tile_p(HEADS * QK_NOPE), tile_p(HEADS * V_HEAD), tile_p(ROPE_LANES)),
        out_shape=(jax.ShapeDtypeStruct((B, T, HEADS * HEAD_PAD), BF16),
                   jax.ShapeDtypeStruct((B, T, HEADS * QK_NOPE), BF16),
                   jax.ShapeDtypeStruct((B, T, HEADS * V_HEAD), BF16),
                   jax.ShapeDtypeStruct((B, T, ROPE_LANES), BF16)),
        compiler_params=params(dimension_semantics=("parallel", "parallel")),
        name="proj",
    )(x, ag, w1, qg, kvg, wuq, wukv, c_r, sa_r, sb_r)

    attn = pl.pallas_call(
        _attn_kernel,
        grid=(B, HEADS),
        in_specs=[pl.BlockSpec((None, T, HEAD_PAD), lambda b, h: (b, 0, h)),
                  pl.BlockSpec((None, T, QK_NOPE), lambda b, h: (b, 0, h)),
                  pl.BlockSpec((None, T, ROPE_LANES), lambda b, h: (b, 0, 0)),
                  pl.BlockSpec((None, T, V_HEAD), lambda b, h: (b, 0, h)),
                  pl.BlockSpec((None, META_PAD, HEAD_PAD), lambda b, h: (h, 0, 0)),
                  pl.BlockSpec((None, META_PAD, V_HEAD + SUM_COLS), lambda b, h: (h, 0, 0))],
        out_specs=pl.BlockSpec((None, T, V_HEAD), lambda b, h: (b, 0, h)),
        out_shape=jax.ShapeDtypeStruct((B, T, HEADS * V_HEAD), BF16),
        scratch_shapes=[pltpu.VMEM((T, HEAD_PAD), BF16),
                        pltpu.VMEM((T, V_HEAD + SUM_COLS), BF16)],
        compiler_params=params(dimension_semantics=("parallel", "parallel")),
        name="attn",
    )(q, k_nope, k_rope, v, kmeta, vmeta)

    n_t = T // TAIL_T
    tail_rows = B * TAIL_T
    tile_t = pl.BlockSpec((B, TAIL_T, D), lambda i: (0, i, 0))
    out = pl.pallas_call(
        _tail_kernel,
        grid=(n_t,),
        in_specs=[tile_t, tile_t, cs((1, D)), cs((D, D)), cs((D, D)), cs((1, D)), cs((D, 2 * LRU_WIDTH)),
                  cs((CONV_WIDTH, LRU_WIDTH)), cs((1, LRU_WIDTH)),
                  cs((LRU_BLOCKS, LRU_BLOCK, 2 * LRU_BLOCK)), cs((1, LRU_WIDTH)), cs((1, LRU_WIDTH)),
                  cs((1, LRU_WIDTH)), cs((LRU_WIDTH, D)), cs((1, D)),
                  cs((CARRY_ROWS, LRU_WIDTH)), cs((1, LRU_WIDTH))],
        out_specs=tile_t,
        out_shape=jax.ShapeDtypeStruct((B, T, D), x.dtype),
        scratch_shapes=[pltpu.VMEM((N_SLAB, (CONV_WIDTH - 1) * B + tail_rows, LANES), F32),
                        pltpu.VMEM((N_SLAB, tail_rows, LANES), F32),
                        pltpu.VMEM((N_SLAB, tail_rows, LANES), F32),
                        pltpu.VMEM((N_SLAB, tail_rows, LANES), F32),
                        pltpu.VMEM((N_SLAB, B, LANES), F32)],
        compiler_params=params(dimension_semantics=("arbitrary",)),
        name="tail",
    )(x, attn, ag, wg, wao, bg, wbin, cw, cb, wgate, brg, big, lam, wbo, fg, utail, h0)
    return out
```

```python
import functools
import math

import jax
import jax.numpy as jnp
from jax import lax
from jax.experimental import pallas as pl
from jax.experimental.pallas import tpu as pltpu

F32 = jnp.float32
BF16 = jnp.bfloat16

D_MODEL = 1024
N_META = 16
RMS_EPS = 1e-6
HEADS = 8
QK_NOPE = 128
QK_ROPE = 64
V_HEAD = 128
Q_LORA = 384
KV_LORA = 256
ROPE_BASE = 10000.0
HEAD_PAD = 256
ROPE_LANES = 128
META_PAD = 128
SUM_COLS = 128
PROJ_COLS = Q_LORA + KV_LORA + ROPE_LANES
LRU_WIDTH = 1024
LRU_BLOCKS = 4
LRU_BLOCK = 256
CONV_WIDTH = 4
LRU_C = 8.0
MASK_VALUE = -1e30
LOG2E = math.log2(math.e)

VMEM_LIMIT_BYTES = 56 * 1024 * 1024

PROJ_ROWS = 1024
ATTN_Q = 512
ATTN_K = 512
TAIL_T = 64
CARRY_ROWS = 8
LANES = 128
SUBLANES = 8
N_SLAB = LRU_WIDTH // LANES


def _rmsnorm(x, g):
    ms = jnp.mean(x * x, axis=-1, keepdims=True)
    return (x * lax.rsqrt(ms + RMS_EPS)) * g


def _rope128(x, c, sa, sb):
    return x * c + pltpu.roll(x, 96, 1) * sa + pltpu.roll(x, 32, 1) * sb


def _sigmoid(x):
    return 0.5 * jnp.tanh(0.5 * x) + 0.5


def _silu(x):
    return x * _sigmoid(x)


def _dot(a, b):
    return jnp.dot(a, b, preferred_element_type=F32)


def _dot_nt(a, b):
    return lax.dot_general(a, b, (((1,), (1,)), ((), ())), preferred_element_type=F32)


def _mla_project(hn_bf16, w1, qg, kvg, wuq, wukv, c, sa, sb):
    proj = _dot(hn_bf16, w1)
    qln = _rmsnorm(proj[:, :Q_LORA], qg).astype(BF16)
    kvn = _rmsnorm(proj[:, Q_LORA:Q_LORA + KV_LORA], kvg).astype(BF16)
    qf = _dot(qln, wuq)
    parts = []
    for h in range(HEADS):
        parts.append(qf[:, h * HEAD_PAD:h * HEAD_PAD + QK_NOPE])
        parts.append(_rope128(qf[:, h * HEAD_PAD + QK_NOPE:(h + 1) * HEAD_PAD], c, sa, sb))
    q = jnp.concatenate(parts, axis=1).astype(BF16)
    kv = _dot(kvn, wukv)
    k_nope = kv[:, :HEADS * QK_NOPE].astype(BF16)
    v = kv[:, HEADS * QK_NOPE:].astype(BF16)
    k_rope = _rope128(proj[:, Q_LORA + KV_LORA:], c, sa, sb).astype(BF16)
    return q, k_nope, v, k_rope


def _lru_terms(r_lin, i_lin, uc, b_rg, b_ig, lam):
    r = _sigmoid(r_lin + b_rg)
    i = _sigmoid(i_lin + b_ig)
    neg_lam = -lam
    softplus = jnp.maximum(neg_lam, 0.0) + jnp.log1p(jnp.exp(-jnp.abs(neg_lam)))
    log_a = (-LRU_C) * r * softplus
    a = jnp.exp(log_a)
    mult = jnp.sqrt(jnp.tanh(-log_a) * (1.0 + a * a))
    return a, mult, i * uc


def _lru_gates(uc, wgate_ref, b_rg, b_ig, lam):
    rs, is_ = [], []
    for g in range(LRU_BLOCKS):
        ri = _dot(uc[:, g * LRU_BLOCK:(g + 1) * LRU_BLOCK].astype(BF16), wgate_ref[g])
        rs.append(ri[:, :LRU_BLOCK])
        is_.append(ri[:, LRU_BLOCK:])
    return _lru_terms(jnp.concatenate(rs, axis=1), jnp.concatenate(is_, axis=1), uc, b_rg, b_ig, lam)


def _meta_kernel(meta_ref, ag_ref, w1_ref, wg_ref, qg_ref, kvg_ref, wuq_ref, wukv_ref, wao_ref,
                 bg_ref, wbu_ref, cw_ref, cb_ref, wgate_ref, brg_ref, big_ref, lam_ref,
                 c_ref, sa_ref, sb_ref,
                 kmeta_ref, vmeta_ref, utail_ref, h0_ref):
    x = meta_ref[...]
    hn = _rmsnorm(x, ag_ref[...]).astype(BF16)
    q, k_nope, v, k_rope = _mla_project(hn, w1_ref[...], qg_ref[...], kvg_ref[...], wuq_ref[...],
                                         wukv_ref[...], c_ref[...], sa_ref[...], sb_ref[...])
    gate = _dot(hn, wg_ref[...])
    row = lax.broadcasted_iota(jnp.int32, (N_META, N_META), 0)
    col = lax.broadcasted_iota(jnp.int32, (N_META, N_META), 1)
    kmeta_ref[...] = jnp.zeros(kmeta_ref.shape, kmeta_ref.dtype)
    vmeta_ref[:, :, :V_HEAD] = jnp.zeros((HEADS, META_PAD, V_HEAD), vmeta_ref.dtype)
    vmeta_ref[:, :, V_HEAD:] = jnp.ones((HEADS, META_PAD, SUM_COLS), vmeta_ref.dtype)
    outs = []
    for h in range(HEADS):
        k_h = jnp.concatenate([k_nope[:, h * QK_NOPE:(h + 1) * QK_NOPE], k_rope], axis=1)
        v_h = v[:, h * V_HEAD:(h + 1) * V_HEAD]
        kmeta_ref[h, :N_META, :] = k_h
        vmeta_ref[h, :N_META, :V_HEAD] = v_h
        s = _dot_nt(q[:, h * HEAD_PAD:(h + 1) * HEAD_PAD], k_h)
        s = jnp.where(col <= row, s, MASK_VALUE)
        p = jnp.exp2(s - jnp.max(s, axis=1, keepdims=True))
        p = p / jnp.sum(p, axis=1, keepdims=True)
        outs.append(_dot(p.astype(BF16), v_h))
    attn = jnp.concatenate(outs, axis=1)
    y = (attn * _silu(gate)).astype(BF16)
    h1 = x + _dot(y, wao_ref[...])

    hn1 = _rmsnorm(h1, bg_ref[...]).astype(BF16)
    u = _dot(hn1, wbu_ref[...])
    trow = lax.broadcasted_iota(jnp.int32, (N_META, LRU_WIDTH), 0)
    uc = cb_ref[...] + u * cw_ref[CONV_WIDTH - 1:CONV_WIDTH, :]
    for k in range(1, CONV_WIDTH):
        shifted = jnp.where(trow >= k, pltpu.roll(u, k, 0), 0.0)
        uc = uc + shifted * cw_ref[CONV_WIDTH - 1 - k:CONV_WIDTH - k, :]
    a, mult, iu = _lru_gates(uc, wgate_ref, brg_ref[...], big_ref[...], lam_ref[...])
    b = jnp.where(trow == 0, 1.0, mult) * iu
    h = jnp.zeros((1, LRU_WIDTH), F32)
    for t in range(N_META):
        h = a[t:t + 1, :] * h + b[t:t + 1, :]
    h0_ref[...] = h
    utail_ref[...] = u[N_META - CARRY_ROWS:, :]


def _proj_kernel(x_ref, ag_ref, w1_ref, qg_ref, kvg_ref, wuq_ref, wukv_ref, c_ref, sa_ref, sb_ref,
                 q_ref, kn_ref, v_ref, kr_ref):
    hn = _rmsnorm(x_ref[...], ag_ref[...]).astype(BF16)
    q, k_nope, v, k_rope = _mla_project(hn, w1_ref[...], qg_ref[...], kvg_ref[...], wuq_ref[...],
                                         wukv_ref[...], c_ref[...], sa_ref[...], sb_ref[...])
    q_ref[...] = q
    kn_ref[...] = k_nope
    v_ref[...] = v
    kr_ref[...] = k_rope


def _attn_kernel(q_ref, kn_ref, kr_ref, v_ref, kmeta_ref, vmeta_ref, o_ref, kcat_ref, vext_ref):
    kcat_ref[:, :QK_NOPE] = kn_ref[...]
    kcat_ref[:, QK_NOPE:] = kr_ref[...]
    vext_ref[:, :V_HEAD] = v_ref[...]
    vext_ref[:, V_HEAD:] = jnp.ones((vext_ref.shape[0], SUM_COLS), vext_ref.dtype)
    n_q = q_ref.shape[0] // ATTN_Q
    row = lax.broadcasted_iota(jnp.int32, (ATTN_Q, ATTN_K), 0)
    col = lax.broadcasted_iota(jnp.int32, (ATTN_Q, ATTN_K), 1)
    diag_ok = col <= row
    meta_ok = lax.broadcasted_iota(jnp.int32, (ATTN_Q, META_PAD), 1) < N_META
    for qi in range(n_q):
        q = q_ref[qi * ATTN_Q:(qi + 1) * ATTN_Q, :]
        m = jnp.full((ATTN_Q, 128), MASK_VALUE, F32)
        acc = jnp.zeros((ATTN_Q, V_HEAD + SUM_COLS), F32)
        for j in range(qi + 1):
            last = j == qi
            s = _dot_nt(q, kcat_ref[j * ATTN_K:(j + 1) * ATTN_K, :])
            if last:
                s = jnp.where(diag_ok, s, MASK_VALUE)
                sm = jnp.where(meta_ok, _dot_nt(q, kmeta_ref[...]), MASK_VALUE)
                rowmax = jnp.maximum(jnp.max(s, axis=1, keepdims=True), jnp.max(sm, axis=1, keepdims=True))
            else:
                rowmax = jnp.max(s, axis=1, keepdims=True)
            m_next = jnp.maximum(m, rowmax)
            p = jnp.exp2(s - jnp.concatenate([m_next] * (ATTN_K // LANES), axis=1)).astype(BF16)
            alpha = jnp.exp2(m - m_next)
            pv = _dot(p, vext_ref[j * ATTN_K:(j + 1) * ATTN_K, :])
            if last:
                pv = pv + _dot(jnp.exp2(sm - m_next).astype(BF16), vmeta_ref[...])
            acc = jnp.concatenate([alpha, alpha], axis=1) * acc + pv
            m = m_next
        out = acc[:, :V_HEAD] / acc[:, V_HEAD:]
        o_ref[qi * ATTN_Q:(qi + 1) * ATTN_Q, :] = out.astype(o_ref.dtype)


def _tail_kernel(x_ref, attn_ref, ag_ref, wg_ref, wao_ref, bg_ref, wbin_ref, cw_ref, cb_ref,
                 wgate_ref, brg_ref, big_ref, lam_ref, wbo_ref, fg_ref, utail_ref, h0_ref,
                 o_ref, uext_ref, a_ref, b_ref, hs_ref, hcar_ref):
    ti = pl.program_id(0)
    nb, tt, d = x_ref.shape
    rows = nb * tt
    carry = (CONV_WIDTH - 1) * nb

    @pl.when(ti == 0)
    def _():
        for g in range(N_SLAB):
            lanes = slice(g * LANES, (g + 1) * LANES)
            for j in range(CONV_WIDTH - 1):
                src = CARRY_ROWS - (CONV_WIDTH - 1) + j
                uext_ref[g, j * nb:(j + 1) * nb, :] = jnp.broadcast_to(utail_ref[src:src + 1, lanes], (nb, LANES))
            hcar_ref[g] = jnp.broadcast_to(h0_ref[:, lanes], (nb, LANES))

    x = x_ref[...].reshape(rows, d)
    hn = _rmsnorm(x, ag_ref[...]).astype(BF16)
    gate = _dot(hn, wg_ref[...])
    y = (attn_ref[...].reshape(rows, d).astype(F32) * _silu(gate)).astype(BF16)
    h1 = x + _dot(y, wao_ref[...])

    hn1 = _rmsnorm(h1, bg_ref[...]).astype(BF16)
    proj = _dot(hn1, wbin_ref[...])
    u = proj[:, :LRU_WIDTH]
    gate_b = proj[:, LRU_WIDTH:]
    for g in range(N_SLAB):
        for b in range(nb):
            uext_ref[g, pl.ds(carry + b, tt, stride=nb), :] = u[b * tt:(b + 1) * tt, g * LANES:(g + 1) * LANES]

    for blk in range(LRU_BLOCKS):
        ucs = []
        for half in range(LRU_BLOCK // LANES):
            g = blk * (LRU_BLOCK // LANES) + half
            lanes = slice(g * LANES, (g + 1) * LANES)
            uc = cb_ref[:, lanes]
            for k in range(CONV_WIDTH):
                uc = uc + uext_ref[g, carry - k * nb:carry - k * nb + rows, :] * cw_ref[CONV_WIDTH - 1 - k:CONV_WIDTH - k, lanes]
            ucs.append(uc)
        ri = _dot(jnp.concatenate(ucs, axis=1).astype(BF16), wgate_ref[blk])
        for half in range(LRU_BLOCK // LANES):
            g = blk * (LRU_BLOCK // LANES) + half
            lanes = slice(g * LANES, (g + 1) * LANES)
            a, mult, iu = _lru_terms(ri[:, half * LANES:(half + 1) * LANES],
                                     ri[:, LRU_BLOCK + half * LANES:LRU_BLOCK + (half + 1) * LANES],
                                     ucs[half], brg_ref[:, lanes], big_ref[:, lanes], lam_ref[:, lanes])
            a_ref[g] = a
            b_ref[g] = mult * iu
    for g in range(N_SLAB):
        uext_ref[g, :carry, :] = uext_ref[g, rows:rows + carry, :]

    def scan(t, hs):
        r = pl.multiple_of(t * nb, nb)
        new = []
        for g in range(N_SLAB):
            h = a_ref[g, pl.ds(r, nb), :] * hs[g] + b_ref[g, pl.ds(r, nb), :]
            hs_ref[g, pl.ds(r, nb), :] = h
            new.append(h)
        return tuple(new)

    h_last = lax.fori_loop(0, tt, scan, tuple(hcar_ref[g] for g in range(N_SLAB)), unroll=8)
    for g in range(N_SLAB):
        hcar_ref[g] = h_last[g]

    hs = jnp.concatenate(
        [jnp.concatenate([hs_ref[g, pl.ds(b, tt, stride=nb), :] for b in range(nb)], axis=0)
         for g in range(N_SLAB)], axis=1)
    y2 = (hs * _silu(gate_b)).astype(BF16)
    h2 = h1 + _dot(y2, wbo_ref[...])
    o_ref[...] = _rmsnorm(h2, fg_ref[...]).reshape(nb, tt, d)


def _const_spec(shape):
    zeros = (0,) * len(shape)
    return pl.BlockSpec(shape, lambda *_: zeros, pipeline_mode=pl.Buffered(1))


def _rope_tables(n_pos):
    pos = jnp.arange(n_pos, dtype=F32)
    inv_freq = ROPE_BASE ** (-jnp.arange(0, QK_ROPE, 2, dtype=F32) / QK_ROPE)
    ang = pos[:, None] * inv_freq[None, :]
    cos, sin = jnp.cos(ang), jnp.sin(ang)
    z = jnp.zeros_like(cos)
    c = jnp.concatenate([cos, cos, z, z], axis=1)
    sa = jnp.concatenate([-sin, z, z, z], axis=1)
    sb = jnp.concatenate([z, sin, z, z], axis=1)
    return c, sa, sb


def kernel(x, meta_tokens, a_norm_g, a_w_in, a_q_norm_g, a_kv_norm_g, a_w_uq, a_w_ukv, a_w_out,
           b_norm_g, b_w_in, b_conv_w, b_conv_b, b_w_rg, b_b_rg, b_w_ig, b_b_ig, b_lam, b_w_out,
           final_norm_g):
    B, T, D = x.shape
    assert D == D_MODEL and T % ATTN_Q == 0 and T % PROJ_ROWS == 0 and T % TAIL_T == 0
    assert B == SUBLANES
    assert ATTN_Q == ATTN_K
    assert a_norm_g.shape[0] == 1 and b_norm_g.shape[0] == 1

    w_in = a_w_in[0]
    n_lat = Q_LORA + KV_LORA + QK_ROPE
    w1 = jnp.concatenate([w_in[:, :n_lat], jnp.zeros((D, ROPE_LANES - QK_ROPE), F32)], axis=1).astype(BF16)
    wg = w_in[:, n_lat:].astype(BF16)
    wuq = a_w_uq[0].reshape(Q_LORA, HEADS, QK_NOPE + QK_ROPE) * ((QK_NOPE + QK_ROPE) ** -0.5 * LOG2E)
    wuq = jnp.pad(wuq, ((0, 0), (0, 0), (0, HEAD_PAD - QK_NOPE - QK_ROPE)))
    wuq = wuq.reshape(Q_LORA, HEADS * HEAD_PAD).astype(BF16)
    wukv = a_w_ukv[0].reshape(KV_LORA, HEADS, QK_NOPE + V_HEAD)
    wukv = jnp.concatenate([wukv[:, :, :QK_NOPE].reshape(KV_LORA, HEADS * QK_NOPE),
                            wukv[:, :, QK_NOPE:].reshape(KV_LORA, HEADS * V_HEAD)], axis=1).astype(BF16)
    wao = a_w_out[0].astype(BF16)
    wbin = b_w_in[0].astype(BF16)
    wgate = jnp.concatenate([b_w_rg[0], b_w_ig[0]], axis=2).astype(BF16)
    wbo = b_w_out[0].astype(BF16)
    row = lambda v: v.reshape(1, -1)
    ag, qg, kvg, bg, fg = row(a_norm_g[0]), row(a_q_norm_g[0]), row(a_kv_norm_g[0]), row(b_norm_g[0]), row(final_norm_g)
    cw, cb = b_conv_w[0], row(b_conv_b[0])
    brg, big, lam = row(b_b_rg[0]), row(b_b_ig[0]), row(b_lam[0])
    c_all, sa_all, sb_all = _rope_tables(N_META + T)
    c_m, sa_m, sb_m = c_all[:N_META], sa_all[:N_META], sb_all[:N_META]
    c_r, sa_r, sb_r = c_all[N_META:], sa_all[N_META:], sb_all[N_META:]

    params = functools.partial(pltpu.CompilerParams, vmem_limit_bytes=VMEM_LIMIT_BYTES)

    kmeta, vmeta, utail, h0 = pl.pallas_call(
        _meta_kernel,
        out_shape=(jax.ShapeDtypeStruct((HEADS, META_PAD, HEAD_PAD), BF16),
                   jax.ShapeDtypeStruct((HEADS, META_PAD, V_HEAD + SUM_COLS), BF16),
                   jax.ShapeDtypeStruct((CARRY_ROWS, LRU_WIDTH), F32),
                   jax.ShapeDtypeStruct((1, LRU_WIDTH), F32)),
        compiler_params=params(),
        name="meta",
    )(meta_tokens, ag, w1, wg, qg, kvg, wuq, wukv, wao, bg, wbin[:, :LRU_WIDTH], cw, cb, wgate,
      brg, big, lam, c_m, sa_m, sb_m)

    n_p = T // PROJ_ROWS
    tile_p = lambda w: pl.BlockSpec((None, PROJ_ROWS, w), lambda b, i: (b, i, 0))
    tab_p = pl.BlockSpec((PROJ_ROWS, ROPE_LANES), lambda b, i: (i, 0))
    cs = _const_spec
    q, k_nope, v, k_rope = pl.pallas_call(
        _proj_kernel,
        grid=(B, n_p),
        in_specs=[tile_p(D), cs((1, D)), cs((D, PROJ_COLS)), cs((1, Q_LORA)), cs((1, KV_LORA)),
                  cs((Q_LORA, HEADS * HEAD_PAD)), cs((KV_LORA, HEADS * (QK_NOPE + V_HEAD))),
                  tab_p, tab_p, tab_p],
        out_specs=(tile_p(HEADS * HEAD_PAD), tile_p(HEADS * QK_NOPE), tile_p(HEADS * V_HEAD), tile_p(ROPE_LANES)),
        out_shape=(jax.ShapeDtypeStruct((B, T, HEADS * HEAD_PAD), BF16),
                   jax.ShapeDtypeStruct((B, T, HEADS * QK_NOPE), BF16),
                   jax.ShapeDtypeStruct((B, T, HEADS * V_HEAD), BF16),
                   jax.ShapeDtypeStruct((B, T, ROPE_LANES), BF16)),
        compiler_params=params(dimension_semantics=("parallel", "parallel")),
        name="proj",
    )(x, ag, w1, qg, kvg, wuq, wukv, c_r, sa_r, sb_r)

    attn = pl.pallas_call(
        _attn_kernel,
        grid=(B, HEADS),
        in_specs=[pl.BlockSpec((None, T, HEAD_PAD), lambda b, h: (b, 0, h)),
                  pl.BlockSpec((None, T, QK_NOPE), lambda b, h: (b, 0, h)),
                  pl.BlockSpec((None, T, ROPE_LANES), lambda b, h: (b, 0, 0)),
                  pl.BlockSpec((None, T, V_HEAD), lambda b, h: (b, 0, h)),
                  pl.BlockSpec((None, META_PAD, HEAD_PAD), lambda b, h: (h, 0, 0)),
                  pl.BlockSpec((None, META_PAD, V_HEAD + SUM_COLS), lambda b, h: (h, 0, 0))],
        out_specs=pl.BlockSpec((None, T, V_HEAD), lambda b, h: (b, 0, h)),
        out_shape=jax.ShapeDtypeStruct((B, T, HEADS * V_HEAD), BF16),
        scratch_shapes=[pltpu.VMEM((T, HEAD_PAD), BF16),
                        pltpu.VMEM((T, V_HEAD + SUM_COLS), BF16)],
        compiler_params=params(dimension_semantics=("parallel", "parallel")),
        name="attn",
    )(q, k_nope, k_rope, v, kmeta, vmeta)

    n_t = T // TAIL_T
    tail_rows = B * TAIL_T
    tile_t = pl.BlockSpec((B, TAIL_T, D), lambda i: (0, i, 0))
    out = pl.pallas_call(
        _tail_kernel,
        grid=(n_t,),
        in_specs=[tile_t, tile_t, cs((1, D)), cs((D, D)), cs((D, D)), cs((1, D)), cs((D, 2 * LRU_WIDTH)),
                  cs((CONV_WIDTH, LRU_WIDTH)), cs((1, LRU_WIDTH)),
                  cs((LRU_BLOCKS, LRU_BLOCK, 2 * LRU_BLOCK)), cs((1, LRU_WIDTH)), cs((1, LRU_WIDTH)),
                  cs((1, LRU_WIDTH)), cs((LRU_WIDTH, D)), cs((1, D)),
                  cs((CARRY_ROWS, LRU_WIDTH)), cs((1, LRU_WIDTH))],
        out_specs=tile_t,
        out_shape=jax.ShapeDtypeStruct((B, T, D), x.dtype),
        scratch_shapes=[pltpu.VMEM((N_SLAB, (CONV_WIDTH - 1) * B + tail_rows, LANES), F32),
                        pltpu.VMEM((N_SLAB, tail_rows, LANES), F32),
                        pltpu.VMEM((N_SLAB, tail_rows, LANES), F32),
                        pltpu.VMEM((N_SLAB, tail_rows, LANES), F32),
                        pltpu.VMEM((N_SLAB, B, LANES), F32)],
        compiler_params=params(dimension_semantics=("arbitrary",)),
        name="tail",
    )(x, attn, ag, wg, wao, bg, wbin, cw, cb, wgate, brg, big, lam, wbo, fg, utail, h0)
    return out
```

```python
import functools
import math

import jax
import jax.numpy as jnp
from jax import lax
from jax.experimental import pallas as pl
from jax.experimental.pallas import tpu as pltpu

F32 = jnp.float32
BF16 = jnp.bfloat16

D_MODEL = 1024
N_META = 16
RMS_EPS = 1e-6
HEADS = 8
QK_NOPE = 128
QK_ROPE = 64
V_HEAD = 128
Q_LORA = 384
KV_LORA = 256
ROPE_BASE = 10000.0
HEAD_PAD = 256
ROPE_LANES = 128
META_PAD = 128
SUM_COLS = 128
PROJ_COLS = Q_LORA + KV_LORA + ROPE_LANES
LRU_WIDTH = 1024
LRU_BLOCKS = 4
LRU_BLOCK = 256
CONV_WIDTH = 4
LRU_C = 8.0
MASK_VALUE = -1e30
LOG2E = math.log2(math.e)

VMEM_LIMIT_BYTES = 56 * 1024 * 1024

PROJ_ROWS = 1024
ATTN_Q = 512
ATTN_K = 512
TAIL_T = 64
CARRY_ROWS = 8
LANES = 128
SUBLANES = 8
N_SLAB = LRU_WIDTH // LANES


def _rmsnorm(x, g):
    ms = jnp.mean(x * x, axis=-1, keepdims=True)
    return (x * lax.rsqrt(ms + RMS_EPS)) * g


def _rope128(x, c, sa, sb):
    return x * c + pltpu.roll(x, 96, 1) * sa + pltpu.roll(x, 32, 1) * sb


def _silu(x):
    hx = 0.5 * x
    return hx * jnp.tanh(hx) + hx


def _dot(a, b):
    return jnp.dot(a, b, preferred_element_type=F32)


def _dot_nt(a, b):
    return lax.dot_general(a, b, (((1,), (1,)), ((), ())), preferred_element_type=F32)


def _mla_project(hn_bf16, w1, qg, kvg, wuq, wukv, c, sa, sb):
    proj = _dot(hn_bf16, w1)
    qln = _rmsnorm(proj[:, :Q_LORA], qg).astype(BF16)
    kvn = _rmsnorm(proj[:, Q_LORA:Q_LORA + KV_LORA], kvg).astype(BF16)
    qf = _dot(qln, wuq)
    parts = []
    for h in range(HEADS):
        parts.append(qf[:, h * HEAD_PAD:h * HEAD_PAD + QK_NOPE])
        parts.append(_rope128(qf[:, h * HEAD_PAD + QK_NOPE:(h + 1) * HEAD_PAD], c, sa, sb))
    q = jnp.concatenate(parts, axis=1).astype(BF16)
    kv = _dot(kvn, wukv)
    k_nope = kv[:, :HEADS * QK_NOPE].astype(BF16)
    v = kv[:, HEADS * QK_NOPE:].astype(BF16)
    k_rope = _rope128(proj[:, Q_LORA + KV_LORA:], c, sa, sb).astype(BF16)
    return q, k_nope, v, k_rope


def _lru_terms(r_half, i_half, uc, b_rg, b_ig, lam):
    tr = jnp.tanh(r_half + 0.5 * b_rg)
    ti = jnp.tanh(i_half + 0.5 * b_ig)
    neg_lam = -lam
    softplus = jnp.maximum(neg_lam, 0.0) + jnp.log1p(jnp.exp(-jnp.abs(neg_lam)))
    half_c = (0.5 * LRU_C) * softplus
    z = half_c * tr + half_c
    a = jnp.exp2(z * (-LOG2E))
    t = jnp.tanh(z) * (1.0 + a * a)
    mult = jnp.where(t > 0.0, t * lax.rsqrt(t), 0.0)
    hu = 0.5 * uc
    return a, mult, hu * ti + hu


def _lru_gates(uc, wgate_ref, b_rg, b_ig, lam):
    rs, is_ = [], []
    for g in range(LRU_BLOCKS):
        ri = _dot(uc[:, g * LRU_BLOCK:(g + 1) * LRU_BLOCK].astype(BF16), wgate_ref[g])
        rs.append(ri[:, :LRU_BLOCK])
        is_.append(ri[:, LRU_BLOCK:])
    return _lru_terms(jnp.concatenate(rs, axis=1), jnp.concatenate(is_, axis=1), uc, b_rg, b_ig, lam)


def _meta_kernel(meta_ref, ag_ref, w1_ref, wg_ref, qg_ref, kvg_ref, wuq_ref, wukv_ref, wao_ref,
                 bg_ref, wbu_ref, cw_ref, cb_ref, wgate_ref, brg_ref, big_ref, lam_ref,
                 c_ref, sa_ref, sb_ref,
                 kmeta_ref, vmeta_ref, utail_ref, h0_ref):
    x = meta_ref[...]
    hn = _rmsnorm(x, ag_ref[...]).astype(BF16)
    q, k_nope, v, k_rope = _mla_project(hn, w1_ref[...], qg_ref[...], kvg_ref[...], wuq_ref[...],
                                         wukv_ref[...], c_ref[...], sa_ref[...], sb_ref[...])
    gate = _dot(hn, wg_ref[...])
    row = lax.broadcasted_iota(jnp.int32, (N_META, N_META), 0)
    col = lax.broadcasted_iota(jnp.int32, (N_META, N_META), 1)
    kmeta_ref[...] = jnp.zeros(kmeta_ref.shape, kmeta_ref.dtype)
    vmeta_ref[:, :, :V_HEAD] = jnp.zeros((HEADS, META_PAD, V_HEAD), vmeta_ref.dtype)
    vmeta_ref[:, :, V_HEAD:] = jnp.ones((HEADS, META_PAD, SUM_COLS), vmeta_ref.dtype)
    outs = []
    for h in range(HEADS):
        k_h = jnp.concatenate([k_nope[:, h * QK_NOPE:(h + 1) * QK_NOPE], k_rope], axis=1)
        v_h = v[:, h * V_HEAD:(h + 1) * V_HEAD]
        kmeta_ref[h, :N_META, :] = k_h
        vmeta_ref[h, :N_META, :V_HEAD] = v_h
        s = _dot_nt(q[:, h * HEAD_PAD:(h + 1) * HEAD_PAD], k_h)
        s = jnp.where(col <= row, s, MASK_VALUE)
        p = jnp.exp2(s - jnp.max(s, axis=1, keepdims=True))
        p = p / jnp.sum(p, axis=1, keepdims=True)
        outs.append(_dot(p.astype(BF16), v_h))
    attn = jnp.concatenate(outs, axis=1)
    y = (attn * _silu(gate)).astype(BF16)
    h1 = x + _dot(y, wao_ref[...])

    hn1 = _rmsnorm(h1, bg_ref[...]).astype(BF16)
    u = _dot(hn1, wbu_ref[...])
    trow = lax.broadcasted_iota(jnp.int32, (N_META, LRU_WIDTH), 0)
    uc = cb_ref[...] + u * cw_ref[CONV_WIDTH - 1:CONV_WIDTH, :]
    for k in range(1, CONV_WIDTH):
        shifted = jnp.where(trow >= k, pltpu.roll(u, k, 0), 0.0)
        uc = uc + shifted * cw_ref[CONV_WIDTH - 1 - k:CONV_WIDTH - k, :]
    a, mult, iu = _lru_gates(uc, wgate_ref, brg_ref[...], big_ref[...], lam_ref[...])
    b = jnp.where(trow == 0, 1.0, mult) * iu
    h = jnp.zeros((1, LRU_WIDTH), F32)
    for t in range(N_META):
        h = a[t:t + 1, :] * h + b[t:t + 1, :]
    h0_ref[...] = h
    utail_ref[...] = u[N_META - CARRY_ROWS:, :]


def _proj_kernel(x_ref, ag_ref, w1_ref, qg_ref, kvg_ref, wuq_ref, wukv_ref, c_ref, sa_ref, sb_ref,
                 q_ref, kn_ref, v_ref, kr_ref):
    hn = _rmsnorm(x_ref[...], ag_ref[...]).astype(BF16)
    q, k_nope, v, k_rope = _mla_project(hn, w1_ref[...], qg_ref[...], kvg_ref[...], wuq_ref[...],
                                         wukv_ref[...], c_ref[...], sa_ref[...], sb_ref[...])
    q_ref[...] = q
    kn_ref[...] = k_nope
    v_ref[...] = v
    kr_ref[...] = k_rope


def _attn_kernel(q_ref, kn_ref, kr_ref, v_ref, kmeta_ref, vmeta_ref, o_ref, kcat_ref, vext_ref):
    kcat_ref[:, :QK_NOPE] = kn_ref[...]
    kcat_ref[:, QK_NOPE:] = kr_ref[...]
    vext_ref[:, :V_HEAD] = v_ref[...]
    vext_ref[:, V_HEAD:] = jnp.ones((vext_ref.shape[0], SUM_COLS), vext_ref.dtype)
    n_q = q_ref.shape[0] // ATTN_Q
    row = lax.broadcasted_iota(jnp.int32, (ATTN_Q, ATTN_K), 0)
    col = lax.broadcasted_iota(jnp.int32, (ATTN_Q, ATTN_K), 1)
    diag_ok = col <= row
    meta_ok = lax.broadcasted_iota(jnp.int32, (ATTN_Q, META_PAD), 1) < N_META
    for qi in range(n_q):
        q = q_ref[qi * ATTN_Q:(qi + 1) * ATTN_Q, :]
        m = jnp.full((ATTN_Q, 128), MASK_VALUE, F32)
        acc = jnp.zeros((ATTN_Q, V_HEAD + SUM_COLS), F32)
        for j in range(qi + 1):
            last = j == qi
            s = _dot_nt(q, kcat_ref[j * ATTN_K:(j + 1) * ATTN_K, :])
            if last:
                s = jnp.where(diag_ok, s, MASK_VALUE)
                sm = jnp.where(meta_ok, _dot_nt(q, kmeta_ref[...]), MASK_VALUE)
                rowmax = jnp.maximum(jnp.max(s, axis=1, keepdims=True), jnp.max(sm, axis=1, keepdims=True))
            else:
                rowmax = jnp.max(s, axis=1, keepdims=True)
            m_next = jnp.maximum(m, rowmax)
            p = jnp.exp2(s - jnp.concatenate([m_next] * (ATTN_K // LANES), axis=1)).astype(BF16)
            alpha = jnp.exp2(m - m_next)
            pv = _dot(p, vext_ref[j * ATTN_K:(j + 1) * ATTN_K, :])
            if last:
                pv = pv + _dot(jnp.exp2(sm - m_next).astype(BF16), vmeta_ref[...])
            acc = jnp.concatenate([alpha, alpha], axis=1) * acc + pv
            m = m_next
        out = acc[:, :V_HEAD] / acc[:, V_HEAD:]
        o_ref[qi * ATTN_Q:(qi + 1) * ATTN_Q, :] = out.astype(o_ref.dtype)


def _tail_kernel(x_ref, attn_ref, ag_ref, wg_ref, wao_ref, bg_ref, wbin_ref, cw_ref, cb_ref,
                 wgate_ref, brg_ref, big_ref, lam_ref, wbo_ref, fg_ref, utail_ref, h0_ref,
                 o_ref, uext_ref, a_ref, b_ref, hs_ref, hcar_ref):
    ti = pl.program_id(0)
    nb, tt, d = x_ref.shape
    rows = nb * tt
    carry = (CONV_WIDTH - 1) * nb

    @pl.when(ti == 0)
    def _():
        for g in range(N_SLAB):
            lanes = slice(g * LANES, (g + 1) * LANES)
            for j in range(CONV_WIDTH - 1):
                src = CARRY_ROWS - (CONV_WIDTH - 1) + j
                uext_ref[g, j * nb:(j + 1) * nb, :] = jnp.broadcast_to(utail_ref[src:src + 1, lanes], (nb, LANES))
            hcar_ref[g] = jnp.broadcast_to(h0_ref[:, lanes], (nb, LANES))

    x = x_ref[...].reshape(rows, d)
    hn = _rmsnorm(x, ag_ref[...]).astype(BF16)
    gate = _dot(hn, wg_ref[...])
    y = (attn_ref[...].reshape(rows, d).astype(F32) * _silu(gate)).astype(BF16)
    h1 = x + _dot(y, wao_ref[...])

    hn1 = _rmsnorm(h1, bg_ref[...]).astype(BF16)
    proj = _dot(hn1, wbin_ref[...])
    u = proj[:, :LRU_WIDTH]
    gate_b = proj[:, LRU_WIDTH:]
    for g in range(N_SLAB):
        for b in range(nb):
            uext_ref[g, pl.ds(carry + b, tt, stride=nb), :] = u[b * tt:(b + 1) * tt, g * LANES:(g + 1) * LANES]

    for blk in range(LRU_BLOCKS):
        ucs = []
        for half in range(LRU_BLOCK // LANES):
            g = blk * (LRU_BLOCK // LANES) + half
            lanes = slice(g * LANES, (g + 1) * LANES)
            uc = cb_ref[:, lanes]
            for k in range(CONV_WIDTH):
                uc = uc + uext_ref[g, carry - k * nb:carry - k * nb + rows, :] * cw_ref[CONV_WIDTH - 1 - k:CONV_WIDTH - k, lanes]
            ucs.append(uc)
        ri = _dot(jnp.concatenate(ucs, axis=1).astype(BF16), wgate_ref[blk])
        for half in range(LRU_BLOCK // LANES):
            g = blk * (LRU_BLOCK // LANES) + half
            lanes = slice(g * LANES, (g + 1) * LANES)
            a, mult, iu = _lru_terms(ri[:, half * LANES:(half + 1) * LANES],
                                     ri[:, LRU_BLOCK + half * LANES:LRU_BLOCK + (half + 1) * LANES],
                                     ucs[half], brg_ref[:, lanes], big_ref[:, lanes], lam_ref[:, lanes])
            a_ref[g] = a
            b_ref[g] = mult * iu
    for g in range(N_SLAB):
        uext_ref[g, :carry, :] = uext_ref[g, rows:rows + carry, :]

    def scan(t, hs):
        r = pl.multiple_of(t * nb, nb)
        new = []
        for g in range(N_SLAB):
            h = a_ref[g, pl.ds(r, nb), :] * hs[g] + b_ref[g, pl.ds(r, nb), :]
            hs_ref[g, pl.ds(r, nb), :] = h
            new.append(h)
        return tuple(new)

    h_last = lax.fori_loop(0, tt, scan, tuple(hcar_ref[g] for g in range(N_SLAB)), unroll=8)
    for g in range(N_SLAB):
        hcar_ref[g] = h_last[g]

    hs = jnp.concatenate(
        [jnp.concatenate([hs_ref[g, pl.ds(b, tt, stride=nb), :] for b in range(nb)], axis=0)
         for g in range(N_SLAB)], axis=1)
    y2 = (hs * _silu(gate_b)).astype(BF16)
    h2 = h1 + _dot(y2, wbo_ref[...])
    o_ref[...] = _rmsnorm(h2, fg_ref[...]).reshape(nb, tt, d)


def _const_spec(shape):
    zeros = (0,) * len(shape)
    return pl.BlockSpec(shape, lambda *_: zeros, pipeline_mode=pl.Buffered(1))


def _rope_tables(n_pos):
    pos = jnp.arange(n_pos, dtype=F32)
    inv_freq = ROPE_BASE ** (-jnp.arange(0, QK_ROPE, 2, dtype=F32) / QK_ROPE)
    ang = pos[:, None] * inv_freq[None, :]
    cos, sin = jnp.cos(ang), jnp.sin(ang)
    z = jnp.zeros_like(cos)
    c = jnp.concatenate([cos, cos, z, z], axis=1)
    sa = jnp.concatenate([-sin, z, z, z], axis=1)
    sb = jnp.concatenate([z, sin, z, z], axis=1)
    return c, sa, sb


def kernel(x, meta_tokens, a_norm_g, a_w_in, a_q_norm_g, a_kv_norm_g, a_w_uq, a_w_ukv, a_w_out,
           b_norm_g, b_w_in, b_conv_w, b_conv_b, b_w_rg, b_b_rg, b_w_ig, b_b_ig, b_lam, b_w_out,
           final_norm_g):
    B, T, D = x.shape
    assert D == D_MODEL and T % ATTN_Q == 0 and T % PROJ_ROWS == 0 and T % TAIL_T == 0
    assert B == SUBLANES
    assert ATTN_Q == ATTN_K
    assert a_norm_g.shape[0] == 1 and b_norm_g.shape[0] == 1

    w_in = a_w_in[0]
    n_lat = Q_LORA + KV_LORA + QK_ROPE
    w1 = jnp.concatenate([w_in[:, :n_lat], jnp.zeros((D, ROPE_LANES - QK_ROPE), F32)], axis=1).astype(BF16)
    wg = w_in[:, n_lat:].astype(BF16)
    wuq = a_w_uq[0].reshape(Q_LORA, HEADS, QK_NOPE + QK_ROPE) * ((QK_NOPE + QK_ROPE) ** -0.5 * LOG2E)
    wuq = jnp.pad(wuq, ((0, 0), (0, 0), (0, HEAD_PAD - QK_NOPE - QK_ROPE)))
    wuq = wuq.reshape(Q_LORA, HEADS * HEAD_PAD).astype(BF16)
    wukv = a_w_ukv[0].reshape(KV_LORA, HEADS, QK_NOPE + V_HEAD)
    wukv = jnp.concatenate([wukv[:, :, :QK_NOPE].reshape(KV_LORA, HEADS * QK_NOPE),
                            wukv[:, :, QK_NOPE:].reshape(KV_LORA, HEADS * V_HEAD)], axis=1).astype(BF16)
    wao = a_w_out[0].astype(BF16)
    wbin = b_w_in[0].astype(BF16)
    wgate = (0.5 * jnp.concatenate([b_w_rg[0], b_w_ig[0]], axis=2)).astype(BF16)
    wbo = b_w_out[0].astype(BF16)
    row = lambda v: v.reshape(1, -1)
    ag, qg, kvg, bg, fg = row(a_norm_g[0]), row(a_q_norm_g[0]), row(a_kv_norm_g[0]), row(b_norm_g[0]), row(final_norm_g)
    cw, cb = b_conv_w[0], row(b_conv_b[0])
    brg, big, lam = row(b_b_rg[0]), row(b_b_ig[0]), row(b_lam[0])
    c_all, sa_all, sb_all = _rope_tables(N_META + T)
    c_m, sa_m, sb_m = c_all[:N_META], sa_all[:N_META], sb_all[:N_META]
    c_r, sa_r, sb_r = c_all[N_META:], sa_all[N_META:], sb_all[N_META:]

    params = functools.partial(pltpu.CompilerParams, vmem_limit_bytes=VMEM_LIMIT_BYTES)

    kmeta, vmeta, utail, h0 = pl.pallas_call(
        _meta_kernel,
        out_shape=(jax.ShapeDtypeStruct((HEADS, META_PAD, HEAD_PAD), BF16),
                   jax.ShapeDtypeStruct((HEADS, META_PAD, V_HEAD + SUM_COLS), BF16),
                   jax.ShapeDtypeStruct((CARRY_ROWS, LRU_WIDTH), F32),
                   jax.ShapeDtypeStruct((1, LRU_WIDTH), F32)),
        compiler_params=params(),
        name="meta",
    )(meta_tokens, ag, w1, wg, qg, kvg, wuq, wukv, wao, bg, wbin[:, :LRU_WIDTH], cw, cb, wgate,
      brg, big, lam, c_m, sa_m, sb_m)

    n_p = T // PROJ_ROWS
    tile_p = lambda w: pl.BlockSpec((None, PROJ_ROWS, w), lambda b, i: (b, i, 0))
    tab_p = pl.BlockSpec((PROJ_ROWS, ROPE_LANES), lambda b, i: (i, 0))
    cs = _const_spec
    q, k_nope, v, k_rope = pl.pallas_call(
        _proj_kernel,
        grid=(B, n_p),
        in_specs=[tile_p(D), cs((1, D)), cs((D, PROJ_COLS)), cs((1, Q_LORA)), cs((1, KV_LORA)),
                  cs((Q_LORA, HEADS * HEAD_PAD)), cs((KV_LORA, HEADS * (QK_NOPE + V_HEAD))),
                  tab_p, tab_p, tab_p],
        out_specs=(tile_p(HEADS * HEAD_PAD), tile_p(HEADS * QK_NOPE), tile_p(HEADS * V_HEAD), tile_p(ROPE_LANES)),
        out_shape=(jax.ShapeDtypeStruct((B, T, HEADS * HEAD_PAD), BF16),
                   jax.ShapeDtypeStruct((B, T, HEADS * QK_NOPE), BF16),
                   jax.ShapeDtypeStruct((B, T, HEADS * V_HEAD), BF16),
                   jax.ShapeDtypeStruct((B, T, ROPE_LANES), BF16)),
        compiler_params=params(dimension_semantics=("parallel", "parallel")),
        name="proj",
    )(x, ag, w1, qg, kvg, wuq, wukv, c_r, sa_r, sb_r)

    attn = pl.pallas_call(
        _attn_kernel,
        grid=(B, HEADS),
        in_specs=[pl.BlockSpec((None, T, HEAD_PAD), lambda b, h: (b, 0, h)),
                  pl.BlockSpec((None, T, QK_NOPE), lambda b, h: (b, 0, h)),
                  pl.BlockSpec((None, T, ROPE_LANES), lambda b, h: (b, 0, 0)),
                  pl.BlockSpec((None, T, V_HEAD), lambda b, h: (b, 0, h)),
                  pl.BlockSpec((None, META_PAD, HEAD_PAD), lambda b, h: (h, 0, 0)),
                  pl.BlockSpec((None, META_PAD, V_HEAD + SUM_COLS), lambda b, h: (h, 0, 0))],
        out_specs=pl.BlockSpec((None, T, V_HEAD), lambda b, h: (b, 0, h)),
        out_shape=jax.ShapeDtypeStruct((B, T, HEADS * V_HEAD), BF16),
        scratch_shapes=[pltpu.VMEM((T, HEAD_PAD), BF16),
                        pltpu.VMEM((T, V_HEAD + SUM_COLS), BF16)],
        compiler_params=params(dimension_semantics=("parallel", "parallel")),
        name="attn",
    )(q, k_nope, k_rope, v, kmeta, vmeta)

    n_t = T // TAIL_T
    tail_rows = B * TAIL_T
    tile_t = pl.BlockSpec((B, TAIL_T, D), lambda i: (0, i, 0))
    out = pl.pallas_call(
        _tail_kernel,
        grid=(n_t,),
        in_specs=[tile_t, tile_t, cs((1, D)), cs((D, D)), cs((D, D)), cs((1, D)), cs((D, 2 * LRU_WIDTH)),
                  cs((CONV_WIDTH, LRU_WIDTH)), cs((1, LRU_WIDTH)),
                  cs((LRU_BLOCKS, LRU_BLOCK, 2 * LRU_BLOCK)), cs((1, LRU_WIDTH)), cs((1, LRU_WIDTH)),
                  cs((1, LRU_WIDTH)), cs((LRU_WIDTH, D)), cs((1, D)),
                  cs((CARRY_ROWS, LRU_WIDTH)), cs((1, LRU_WIDTH))],
        out_specs=tile_t,
        out_shape=jax.ShapeDtypeStruct((B, T, D), x.dtype),
        scratch_shapes=[pltpu.VMEM((N_SLAB, (CONV_WIDTH - 1) * B + tail_rows, LANES), F32),
                        pltpu.VMEM((N_SLAB, tail_rows, LANES), F32),
                        pltpu.VMEM((N_SLAB, tail_rows, LANES), F32),
                        pltpu.VMEM((N_SLAB, tail_rows, LANES), F32),
                        pltpu.VMEM((N_SLAB, B, LANES), F32)],
        compiler_params=params(dimension_semantics=("arbitrary",)),
        name="tail",
    )(x, attn, ag, wg, wao, bg, wbin, cw, cb, wgate, brg, big, lam, wbo, fg, utail, h0)
    return out
```

```python
import functools
import math

import jax
import jax.numpy as jnp
from jax import lax
from jax.experimental import pallas as pl
from jax.experimental.pallas import tpu as pltpu

F32 = jnp.float32
BF16 = jnp.bfloat16

D_MODEL = 1024
N_META = 16
RMS_EPS = 1e-6
HEADS = 8
QK_NOPE = 128
QK_ROPE = 64
V_HEAD = 128
Q_LORA = 384
KV_LORA = 256
ROPE_BASE = 10000.0
HEAD_PAD = 256
ROPE_LANES = 128
META_PAD = 128
SUM_COLS = 128
PROJ_COLS = Q_LORA + KV_LORA + ROPE_LANES
LRU_WIDTH = 1024
LRU_BLOCKS = 4
LRU_BLOCK = 256
CONV_WIDTH = 4
LRU_C = 8.0
MASK_VALUE = -1e30
LOG2E = math.log2(math.e)

VMEM_LIMIT_BYTES = 56 * 1024 * 1024

PROJ_ROWS = 1024
ATTN_Q = 512
ATTN_K = 512
TAIL_T = 64
CARRY_ROWS = 8
LANES = 128
SUBLANES = 8
N_SLAB = LRU_WIDTH // LANES


def _rmsnorm(x, g):
    ms = jnp.mean(x * x, axis=-1, keepdims=True)
    return (x * lax.rsqrt(ms + RMS_EPS)) * g


def _rope128(x, c, sa, sb):
    return x * c + pltpu.roll(x, 96, 1) * sa + pltpu.roll(x, 32, 1) * sb


def _silu(x):
    hx = 0.5 * x
    return hx * jnp.tanh(hx) + hx


def _dot(a, b):
    return jnp.dot(a, b, preferred_element_type=F32)


def _dot_nt(a, b):
    return lax.dot_general(a, b, (((1,), (1,)), ((), ())), preferred_element_type=F32)


def _mla_project(hn_bf16, w1, qg, kvg, wuq, wukv, c, sa, sb):
    proj = _dot(hn_bf16, w1)
    qln = _rmsnorm(proj[:, :Q_LORA], qg).astype(BF16)
    kvn = _rmsnorm(proj[:, Q_LORA:Q_LORA + KV_LORA], kvg).astype(BF16)
    qf = _dot(qln, wuq)
    parts = []
    for h in range(HEADS):
        parts.append(qf[:, h * HEAD_PAD:h * HEAD_PAD + QK_NOPE])
        parts.append(_rope128(qf[:, h * HEAD_PAD + QK_NOPE:(h + 1) * HEAD_PAD], c, sa, sb))
    q = jnp.concatenate(parts, axis=1).astype(BF16)
    kv = _dot(kvn, wukv)
    k_nope = kv[:, :HEADS * QK_NOPE].astype(BF16)
    v = kv[:, HEADS * QK_NOPE:].astype(BF16)
    k_rope = _rope128(proj[:, Q_LORA + KV_LORA:], c, sa, sb).astype(BF16)
    return q, k_nope, v, k_rope


def _lru_terms(r_half, i_half, uc, b_rg, b_ig, lam):
    tr = jnp.tanh(r_half + 0.5 * b_rg)
    ti = jnp.tanh(i_half + 0.5 * b_ig)
    neg_lam = -lam
    softplus = jnp.maximum(neg_lam, 0.0) + jnp.log1p(jnp.exp(-jnp.abs(neg_lam)))
    half_c = (0.5 * LRU_C) * softplus
    z = half_c * tr + half_c
    a = jnp.exp2(z * (-LOG2E))
    t = jnp.tanh(z) * (1.0 + a * a)
    mult = jnp.where(t > 0.0, t * lax.rsqrt(t), 0.0)
    hu = 0.5 * uc
    return a, mult, hu * ti + hu


def _lru_gates(uc, wgate_ref, b_rg, b_ig, lam):
    rs, is_ = [], []
    for g in range(LRU_BLOCKS):
        ri = _dot(uc[:, g * LRU_BLOCK:(g + 1) * LRU_BLOCK].astype(BF16), wgate_ref[g])
        rs.append(ri[:, :LRU_BLOCK])
        is_.append(ri[:, LRU_BLOCK:])
    return _lru_terms(jnp.concatenate(rs, axis=1), jnp.concatenate(is_, axis=1), uc, b_rg, b_ig, lam)


def _meta_kernel(meta_ref, ag_ref, w1_ref, wg_ref, qg_ref, kvg_ref, wuq_ref, wukv_ref, wao_ref,
                 bg_ref, wbu_ref, cw_ref, cb_ref, wgate_ref, brg_ref, big_ref, lam_ref,
                 c_ref, sa_ref, sb_ref,
                 kmeta_ref, vmeta_ref, utail_ref, h0_ref):
    x = meta_ref[...]
    hn = _rmsnorm(x, ag_ref[...]).astype(BF16)
    q, k_nope, v, k_rope = _mla_project(hn, w1_ref[...], qg_ref[...], kvg_ref[...], wuq_ref[...],
                                         wukv_ref[...], c_ref[...], sa_ref[...], sb_ref[...])
    gate = _dot(hn, wg_ref[...])
    row = lax.broadcasted_iota(jnp.int32, (N_META, N_META), 0)
    col = lax.broadcasted_iota(jnp.int32, (N_META, N_META), 1)
    kmeta_ref[...] = jnp.zeros(kmeta_ref.shape, kmeta_ref.dtype)
    vmeta_ref[:, :, :V_HEAD] = jnp.zeros((HEADS, META_PAD, V_HEAD), vmeta_ref.dtype)
    vmeta_ref[:, :, V_HEAD:] = jnp.ones((HEADS, META_PAD, SUM_COLS), vmeta_ref.dtype)
    outs = []
    for h in range(HEADS):
        k_h = jnp.concatenate([k_nope[:, h * QK_NOPE:(h + 1) * QK_NOPE], k_rope], axis=1)
        v_h = v[:, h * V_HEAD:(h + 1) * V_HEAD]
        kmeta_ref[h, :N_META, :] = k_h
        vmeta_ref[h, :N_META, :V_HEAD] = v_h
        s = _dot_nt(q[:, h * HEAD_PAD:(h + 1) * HEAD_PAD], k_h)
        s = jnp.where(col <= row, s, MASK_VALUE)
        p = jnp.exp2(s - jnp.max(s, axis=1, keepdims=True))
        p = p / jnp.sum(p, axis=1, keepdims=True)
        outs.append(_dot(p.astype(BF16), v_h))
    attn = jnp.concatenate(outs, axis=1)
    y = (attn * _silu(gate)).astype(BF16)
    h1 = x + _dot(y, wao_ref[...])

    hn1 = _rmsnorm(h1, bg_ref[...]).astype(BF16)
    u = _dot(hn1, wbu_ref[...])
    trow = lax.broadcasted_iota(jnp.int32, (N_META, LRU_WIDTH), 0)
    uc = cb_ref[...] + u * cw_ref[CONV_WIDTH - 1:CONV_WIDTH, :]
    for k in range(1, CONV_WIDTH):
        shifted = jnp.where(trow >= k, pltpu.roll(u, k, 0), 0.0)
        uc = uc + shifted * cw_ref[CONV_WIDTH - 1 - k:CONV_WIDTH - k, :]
    a, mult, iu = _lru_gates(uc, wgate_ref, brg_ref[...], big_ref[...], lam_ref[...])
    b = jnp.where(trow == 0, 1.0, mult) * iu
    h = jnp.zeros((1, LRU_WIDTH), F32)
    for t in range(N_META):
        h = a[t:t + 1, :] * h + b[t:t + 1, :]
    h0_ref[...] = h
    utail_ref[...] = u[N_META - CARRY_ROWS:, :]


def _proj_kernel(x_ref, ag_ref, w1_ref, qg_ref, kvg_ref, wuq_ref, wukv_ref, c_ref, sa_ref, sb_ref,
                 q_ref, kn_ref, v_ref, kr_ref):
    hn = _rmsnorm(x_ref[...], ag_ref[...]).astype(BF16)
    q, k_nope, v, k_rope = _mla_project(hn, w1_ref[...], qg_ref[...], kvg_ref[...], wuq_ref[...],
                                         wukv_ref[...], c_ref[...], sa_ref[...], sb_ref[...])
    q_ref[...] = q
    kn_ref[...] = k_nope
    v_ref[...] = v
    kr_ref[...] = k_rope


_DONE = object()


def _attn_kernel(q_ref, kn_ref, kr_ref, v_ref, kmeta_ref, vmeta_ref, o_ref, kcat_ref, vext_ref):
    kcat_ref[:, :QK_NOPE] = kn_ref[...]
    kcat_ref[:, QK_NOPE:] = kr_ref[...]
    vext_ref[:, :V_HEAD] = v_ref[...]
    vext_ref[:, V_HEAD:] = jnp.ones((vext_ref.shape[0], SUM_COLS), vext_ref.dtype)
    n_q = q_ref.shape[0] // ATTN_Q
    row = lax.broadcasted_iota(jnp.int32, (ATTN_Q, ATTN_K), 0)
    col = lax.broadcasted_iota(jnp.int32, (ATTN_Q, ATTN_K), 1)
    diag_ok = col <= row
    meta_ok = lax.broadcasted_iota(jnp.int32, (ATTN_Q, META_PAD), 1) < N_META
    def tile(qi):
        q = q_ref[qi * ATTN_Q:(qi + 1) * ATTN_Q, :]
        s = jnp.where(diag_ok, _dot_nt(q, kcat_ref[qi * ATTN_K:(qi + 1) * ATTN_K, :]), MASK_VALUE)
        sm = jnp.where(meta_ok, _dot_nt(q, kmeta_ref[...]), MASK_VALUE)
        m = jnp.broadcast_to(jnp.maximum(jnp.max(s, axis=1, keepdims=True),
                                         jnp.max(sm, axis=1, keepdims=True)), (ATTN_Q, LANES))
        p = jnp.exp2(s - jnp.concatenate([m] * (ATTN_K // LANES), axis=1)).astype(BF16)
        acc = (_dot(p, vext_ref[qi * ATTN_K:(qi + 1) * ATTN_K, :])
               + _dot(jnp.exp2(sm - m).astype(BF16), vmeta_ref[...]))
        yield
        for j in range(qi):
            s = _dot_nt(q, kcat_ref[j * ATTN_K:(j + 1) * ATTN_K, :])
            m_next = jnp.maximum(m, jnp.max(s, axis=1, keepdims=True))
            p = jnp.exp2(s - jnp.concatenate([m_next] * (ATTN_K // LANES), axis=1)).astype(BF16)
            alpha = jnp.exp2(m - m_next)
            acc = (jnp.concatenate([alpha, alpha], axis=1) * acc
                   + _dot(p, vext_ref[j * ATTN_K:(j + 1) * ATTN_K, :]))
            m = m_next
            yield
        out = acc[:, :V_HEAD] / acc[:, V_HEAD:]
        o_ref[qi * ATTN_Q:(qi + 1) * ATTN_Q, :] = out.astype(o_ref.dtype)

    live = [tile(i) for lo in range(n_q // 2) for i in (lo, n_q - 1 - lo)]
    while live:
        live = [g for g in live if next(g, _DONE) is not _DONE]


def _tail_kernel(x_ref, attn_ref, ag_ref, wg_ref, wao_ref, bg_ref, wbin_ref, cw_ref, cb_ref,
                 wgate_ref, brg_ref, big_ref, lam_ref, wbo_ref, fg_ref, utail_ref, h0_ref,
                 o_ref, uext_ref, a_ref, b_ref, hs_ref, hcar_ref):
    ti = pl.program_id(0)
    nb, tt, d = x_ref.shape
    rows = nb * tt
    carry = (CONV_WIDTH - 1) * nb

    @pl.when(ti == 0)
    def _():
        for g in range(N_SLAB):
            lanes = slice(g * LANES, (g + 1) * LANES)
            for j in range(CONV_WIDTH - 1):
                src = CARRY_ROWS - (CONV_WIDTH - 1) + j
                uext_ref[g, j * nb:(j + 1) * nb, :] = jnp.broadcast_to(utail_ref[src:src + 1, lanes], (nb, LANES))
            hcar_ref[g] = jnp.broadcast_to(h0_ref[:, lanes], (nb, LANES))

    x = x_ref[...].reshape(rows, d)
    hn = _rmsnorm(x, ag_ref[...]).astype(BF16)
    gate = _dot(hn, wg_ref[...])
    y = (attn_ref[...].reshape(rows, d).astype(F32) * _silu(gate)).astype(BF16)
    h1 = x + _dot(y, wao_ref[...])

    hn1 = _rmsnorm(h1, bg_ref[...]).astype(BF16)
    proj = _dot(hn1, wbin_ref[...])
    u = proj[:, :LRU_WIDTH]
    gate_b = proj[:, LRU_WIDTH:]
    for g in range(N_SLAB):
        for b in range(nb):
            uext_ref[g, pl.ds(carry + b, tt, stride=nb), :] = u[b * tt:(b + 1) * tt, g * LANES:(g + 1) * LANES]

    for blk in range(LRU_BLOCKS):
        ucs = []
        for half in range(LRU_BLOCK // LANES):
            g = blk * (LRU_BLOCK // LANES) + half
            lanes = slice(g * LANES, (g + 1) * LANES)
            uc = cb_ref[:, lanes]
            for k in range(CONV_WIDTH):
                uc = uc + uext_ref[g, carry - k * nb:carry - k * nb + rows, :] * cw_ref[CONV_WIDTH - 1 - k:CONV_WIDTH - k, lanes]
            ucs.append(uc)
        ri = _dot(jnp.concatenate(ucs, axis=1).astype(BF16), wgate_ref[blk])
        for half in range(LRU_BLOCK // LANES):
            g = blk * (LRU_BLOCK // LANES) + half
            lanes = slice(g * LANES, (g + 1) * LANES)
            a, mult, iu = _lru_terms(ri[:, half * LANES:(half + 1) * LANES],
                                     ri[:, LRU_BLOCK + half * LANES:LRU_BLOCK + (half + 1) * LANES],
                                     ucs[half], brg_ref[:, lanes], big_ref[:, lanes], lam_ref[:, lanes])
            a_ref[g] = a
            b_ref[g] = mult * iu
    for g in range(N_SLAB):
        uext_ref[g, :carry, :] = uext_ref[g, rows:rows + carry, :]

    def scan(t, hs):
        r = pl.multiple_of(t * nb, nb)
        new = []
        for g in range(N_SLAB):
            h = a_ref[g, pl.ds(r, nb), :] * hs[g] + b_ref[g, pl.ds(r, nb), :]
            hs_ref[g, pl.ds(r, nb), :] = h
            new.append(h)
        return tuple(new)

    h_last = lax.fori_loop(0, tt, scan, tuple(hcar_ref[g] for g in range(N_SLAB)), unroll=8)
    for g in range(N_SLAB):
        hcar_ref[g] = h_last[g]

    hs = jnp.concatenate(
        [jnp.concatenate([hs_ref[g, pl.ds(b, tt, stride=nb), :] for b in range(nb)], axis=0)
         for g in range(N_SLAB)], axis=1)
    y2 = (hs * _silu(gate_b)).astype(BF16)
    h2 = h1 + _dot(y2, wbo_ref[...])
    o_ref[...] = _rmsnorm(h2, fg_ref[...]).reshape(nb, tt, d)


def _const_spec(shape):
    zeros = (0,) * len(shape)
    return pl.BlockSpec(shape, lambda *_: zeros, pipeline_mode=pl.Buffered(1))


def _rope_tables(n_pos):
    pos = jnp.arange(n_pos, dtype=F32)
    inv_freq = ROPE_BASE ** (-jnp.arange(0, QK_ROPE, 2, dtype=F32) / QK_ROPE)
    ang = pos[:, None] * inv_freq[None, :]
    cos, sin = jnp.cos(ang), jnp.sin(ang)
    z = jnp.zeros_like(cos)
    c = jnp.concatenate([cos, cos, z, z], axis=1)
    sa = jnp.concatenate([-sin, z, z, z], axis=1)
    sb = jnp.concatenate([z, sin, z, z], axis=1)
    return c, sa, sb


def kernel(x, meta_tokens, a_norm_g, a_w_in, a_q_norm_g, a_kv_norm_g, a_w_uq, a_w_ukv, a_w_out,
           b_norm_g, b_w_in, b_conv_w, b_conv_b, b_w_rg, b_b_rg, b_w_ig, b_b_ig, b_lam, b_w_out,
           final_norm_g):
    B, T, D = x.shape
    assert D == D_MODEL and T % ATTN_Q == 0 and T % PROJ_ROWS == 0 and T % TAIL_T == 0
    assert B == SUBLANES
    assert ATTN_Q == ATTN_K and (T // ATTN_Q) % 2 == 0
    assert a_norm_g.shape[0] == 1 and b_norm_g.shape[0] == 1

    w_in = a_w_in[0]
    n_lat = Q_LORA + KV_LORA + QK_ROPE
    w1 = jnp.concatenate([w_in[:, :n_lat], jnp.zeros((D, ROPE_LANES - QK_ROPE), F32)], axis=1).astype(BF16)
    wg = w_in[:, n_lat:].astype(BF16)
    wuq = a_w_uq[0].reshape(Q_LORA, HEADS, QK_NOPE + QK_ROPE) * ((QK_NOPE + QK_ROPE) ** -0.5 * LOG2E)
    wuq = jnp.pad(wuq, ((0, 0), (0, 0), (0, HEAD_PAD - QK_NOPE - QK_ROPE)))
    wuq = wuq.reshape(Q_LORA, HEADS * HEAD_PAD).astype(BF16)
    wukv = a_w_ukv[0].reshape(KV_LORA, HEADS, QK_NOPE + V_HEAD)
    wukv = jnp.concatenate([wukv[:, :, :QK_NOPE].reshape(KV_LORA, HEADS * QK_NOPE),
                            wukv[:, :, QK_NOPE:].reshape(KV_LORA, HEADS * V_HEAD)], axis=1).astype(BF16)
    wao = a_w_out[0].astype(BF16)
    wbin = b_w_in[0].astype(BF16)
    wgate = (0.5 * jnp.concatenate([b_w_rg[0], b_w_ig[0]], axis=2)).astype(BF16)
    wbo = b_w_out[0].astype(BF16)
    row = lambda v: v.reshape(1, -1)
    ag, qg, kvg, bg, fg = row(a_norm_g[0]), row(a_q_norm_g[0]), row(a_kv_norm_g[0]), row(b_norm_g[0]), row(final_norm_g)
    cw, cb = b_conv_w[0], row(b_conv_b[0])
    brg, big, lam = row(b_b_rg[0]), row(b_b_ig[0]), row(b_lam[0])
    c_all, sa_all, sb_all = _rope_tables(N_META + T)
    c_m, sa_m, sb_m = c_all[:N_META], sa_all[:N_META], sb_all[:N_META]
    c_r, sa_r, sb_r = c_all[N_META:], sa_all[N_META:], sb_all[N_META:]

    params = functools.partial(pltpu.CompilerParams, vmem_limit_bytes=VMEM_LIMIT_BYTES)

    kmeta, vmeta, utail, h0 = pl.pallas_call(
        _meta_kernel,
        out_shape=(jax.ShapeDtypeStruct((HEADS, META_PAD, HEAD_PAD), BF16),
                   jax.ShapeDtypeStruct((HEADS, META_PAD, V_HEAD + SUM_COLS), BF16),
                   jax.ShapeDtypeStruct((CARRY_ROWS, LRU_WIDTH), F32),
                   jax.ShapeDtypeStruct((1, LRU_WIDTH), F32)),
        compiler_params=params(),
        name="meta",
    )(meta_tokens, ag, w1, wg, qg, kvg, wuq, wukv, wao, bg, wbin[:, :LRU_WIDTH], cw, cb, wgate,
      brg, big, lam, c_m, sa_m, sb_m)

    n_p = T // PROJ_ROWS
    tile_p = lambda w: pl.BlockSpec((None, PROJ_ROWS, w), lambda b, i: (b, i, 0))
    tab_p = pl.BlockSpec((PROJ_ROWS, ROPE_LANES), lambda b, i: (i, 0))
    cs = _const_spec
    q, k_nope, v, k_rope = pl.pallas_call(
        _proj_kernel,
        grid=(B, n_p),
        in_specs=[tile_p(D), cs((1, D)), cs((D, PROJ_COLS)), cs((1, Q_LORA)), cs((1, KV_LORA)),
                  cs((Q_LORA, HEADS * HEAD_PAD)), cs((KV_LORA, HEADS * (QK_NOPE + V_HEAD))),
                  tab_p, tab_p, tab_p],
        out_specs=(tile_p(HEADS * HEAD_PAD), tile_p(HEADS * QK_NOPE), tile_p(HEADS * V_HEAD), tile_p(ROPE_LANES)),
        out_shape=(jax.ShapeDtypeStruct((B, T, HEADS * HEAD_PAD), BF16),
                   jax.ShapeDtypeStruct((B, T, HEADS * QK_NOPE), BF16),
                   jax.ShapeDtypeStruct((B, T, HEADS * V_HEAD), BF16),
                   jax.ShapeDtypeStruct((B, T, ROPE_LANES), BF16)),
        compiler_params=params(dimension_semantics=("parallel", "parallel")),
        name="proj",
    )(x, ag, w1, qg, kvg, wuq, wukv, c_r, sa_r, sb_r)

    attn = pl.pallas_call(
        _attn_kernel,
        grid=(B, HEADS),
        in_specs=[pl.BlockSpec((None, T, HEAD_PAD), lambda b, h: (b, 0, h)),
                  pl.BlockSpec((None, T, QK_NOPE), lambda b, h: (b, 0, h)),
                  pl.BlockSpec((None, T, ROPE_LANES), lambda b, h: (b, 0, 0)),
                  pl.BlockSpec((None, T, V_HEAD), lambda b, h: (b, 0, h)),
                  pl.BlockSpec((None, META_PAD, HEAD_PAD), lambda b, h: (h, 0, 0)),
                  pl.BlockSpec((None, META_PAD, V_HEAD + SUM_COLS), lambda b, h: (h, 0, 0))],
        out_specs=pl.BlockSpec((None, T, V_HEAD), lambda b, h: (b, 0, h)),
        out_shape=jax.ShapeDtypeStruct((B, T, HEADS * V_HEAD), BF16),
        scratch_shapes=[pltpu.VMEM((T, HEAD_PAD), BF16),
                        pltpu.VMEM((T, V_HEAD + SUM_COLS), BF16)],
        compiler_params=params(dimension_semantics=("parallel", "parallel")),
        name="attn",
    )(q, k_nope, k_rope, v, kmeta, vmeta)

    n_t = T // TAIL_T
    tail_rows = B * TAIL_T
    tile_t = pl.BlockSpec((B, TAIL_T, D), lambda i: (0, i, 0))
    out = pl.pallas_call(
        _tail_kernel,
        grid=(n_t,),
        in_specs=[tile_t, tile_t, cs((1, D)), cs((D, D)), cs((D, D)), cs((1, D)), cs((D, 2 * LRU_WIDTH)),
                  cs((CONV_WIDTH, LRU_WIDTH)), cs((1, LRU_WIDTH)),
                  cs((LRU_BLOCKS, LRU_BLOCK, 2 * LRU_BLOCK)), cs((1, LRU_WIDTH)), cs((1, LRU_WIDTH)),
                  cs((1, LRU_WIDTH)), cs((LRU_WIDTH, D)), cs((1, D)),
                  cs((CARRY_ROWS, LRU_WIDTH)), cs((1, LRU_WIDTH))],
        out_specs=tile_t,
        out_shape=jax.ShapeDtypeStruct((B, T, D), x.dtype),
        scratch_shapes=[pltpu.VMEM((N_SLAB, (CONV_WIDTH - 1) * B + tail_rows, LANES), F32),
                        pltpu.VMEM((N_SLAB, tail_rows, LANES), F32),
                        pltpu.VMEM((N_SLAB, tail_rows, LANES), F32),
                        pltpu.VMEM((N_SLAB, tail_rows, LANES), F32),
                        pltpu.VMEM((N_SLAB, B, LANES), F32)],
        compiler_params=params(dimension_semantics=("arbitrary",)),
        name="tail",
    )(x, attn, ag, wg, wao, bg, wbin, cw, cb, wgate, brg, big, lam, wbo, fg, utail, h0)
    return out
```

```python
import functools
import math

import jax
import jax.numpy as jnp
from jax import lax
from jax.experimental import pallas as pl
from jax.experimental.pallas import tpu as pltpu

F32 = jnp.float32
BF16 = jnp.bfloat16

D_MODEL = 1024
N_META = 16
RMS_EPS = 1e-6
HEADS = 8
QK_NOPE = 128
QK_ROPE = 64
V_HEAD = 128
Q_LORA = 384
KV_LORA = 256
ROPE_BASE = 10000.0
HEAD_PAD = 256
ROPE_LANES = 128
META_PAD = 128
SUM_COLS = 128
PROJ_COLS = Q_LORA + KV_LORA + ROPE_LANES
LRU_WIDTH = 1024
LRU_BLOCKS = 4
LRU_BLOCK = 256
CONV_WIDTH = 4
LRU_C = 8.0
MASK_VALUE = -1e30
LOG2E = math.log2(math.e)

VMEM_LIMIT_BYTES = 56 * 1024 * 1024

PROJ_ROWS = 1024
ATTN_Q = 512
ATTN_K = 512
TAIL_T = 64
CARRY_ROWS = 8
LANES = 128
SUBLANES = 8
N_SLAB = LRU_WIDTH // LANES


def _rmsnorm(x, g):
    ms = jnp.mean(x * x, axis=-1, keepdims=True)
    return (x * lax.rsqrt(ms + RMS_EPS)) * g


def _rope128(x, c, sa, sb):
    return x * c + pltpu.roll(x, 96, 1) * sa + pltpu.roll(x, 32, 1) * sb


_DONE = object()


def _run_round_robin(gens):
    live = list(gens)
    while live:
        live = [g for g in live if next(g, _DONE) is not _DONE]


def _silu(x):
    hx = 0.5 * x
    return hx * jnp.tanh(hx) + hx


def _dot(a, b):
    return jnp.dot(a, b, preferred_element_type=F32)


def _dot_nt(a, b):
    return lax.dot_general(a, b, (((1,), (1,)), ((), ())), preferred_element_type=F32)


def _mla_project(hn_bf16, w1, qg, kvg, wuq, wukv, c, sa, sb):
    proj = _dot(hn_bf16, w1)
    qln = _rmsnorm(proj[:, :Q_LORA], qg).astype(BF16)
    kvn = _rmsnorm(proj[:, Q_LORA:Q_LORA + KV_LORA], kvg).astype(BF16)
    qf = _dot(qln, wuq)
    parts = []
    for h in range(HEADS):
        parts.append(qf[:, h * HEAD_PAD:h * HEAD_PAD + QK_NOPE])
        parts.append(_rope128(qf[:, h * HEAD_PAD + QK_NOPE:(h + 1) * HEAD_PAD], c, sa, sb))
    q = jnp.concatenate(parts, axis=1).astype(BF16)
    kv = _dot(kvn, wukv)
    k_nope = kv[:, :HEADS * QK_NOPE].astype(BF16)
    v = kv[:, HEADS * QK_NOPE:].astype(BF16)
    k_rope = _rope128(proj[:, Q_LORA + KV_LORA:], c, sa, sb).astype(BF16)
    return q, k_nope, v, k_rope


def _lru_terms(r_half, i_half, uc, b_rg, b_ig, lam):
    tr = jnp.tanh(r_half + 0.5 * b_rg)
    ti = jnp.tanh(i_half + 0.5 * b_ig)
    neg_lam = -lam
    softplus = jnp.maximum(neg_lam, 0.0) + jnp.log1p(jnp.exp(-jnp.abs(neg_lam)))
    half_c = (0.5 * LRU_C) * softplus
    z = half_c * tr + half_c
    a = jnp.exp2(z * (-LOG2E))
    t = jnp.tanh(z) * (1.0 + a * a)
    mult = jnp.where(t > 0.0, t * lax.rsqrt(t), 0.0)
    hu = 0.5 * uc
    return a, mult, hu * ti + hu


def _lru_gates(uc, wgate_ref, b_rg, b_ig, lam):
    rs, is_ = [], []
    for g in range(LRU_BLOCKS):
        ri = _dot(uc[:, g * LRU_BLOCK:(g + 1) * LRU_BLOCK].astype(BF16), wgate_ref[g])
        rs.append(ri[:, :LRU_BLOCK])
        is_.append(ri[:, LRU_BLOCK:])
    return _lru_terms(jnp.concatenate(rs, axis=1), jnp.concatenate(is_, axis=1), uc, b_rg, b_ig, lam)


def _meta_kernel(meta_ref, ag_ref, w1_ref, wg_ref, qg_ref, kvg_ref, wuq_ref, wukv_ref, wao_ref,
                 bg_ref, wbin_ref, cw_ref, cb_ref, wgate_ref, brg_ref, big_ref, lam_ref,
                 c_ref, sa_ref, sb_ref,
                 kmeta_ref, vmeta_ref, utail_ref, h0_ref):
    x = meta_ref[...]
    hn = _rmsnorm(x, ag_ref[...]).astype(BF16)
    q, k_nope, v, k_rope = _mla_project(hn, w1_ref[...], qg_ref[...], kvg_ref[...], wuq_ref[...],
                                         wukv_ref[...], c_ref[...], sa_ref[...], sb_ref[...])
    gate = _dot(hn, wg_ref[...])
    row = lax.broadcasted_iota(jnp.int32, (N_META, N_META), 0)
    col = lax.broadcasted_iota(jnp.int32, (N_META, N_META), 1)
    kmeta_ref[...] = jnp.zeros(kmeta_ref.shape, kmeta_ref.dtype)
    vmeta_ref[:, :, :V_HEAD] = jnp.zeros((HEADS, META_PAD, V_HEAD), vmeta_ref.dtype)
    vmeta_ref[:, :, V_HEAD:] = jnp.ones((HEADS, META_PAD, SUM_COLS), vmeta_ref.dtype)
    outs = []
    for h in range(HEADS):
        k_h = jnp.concatenate([k_nope[:, h * QK_NOPE:(h + 1) * QK_NOPE], k_rope], axis=1)
        v_h = v[:, h * V_HEAD:(h + 1) * V_HEAD]
        kmeta_ref[h, :N_META, :] = k_h
        vmeta_ref[h, :N_META, :V_HEAD] = v_h
        s = _dot_nt(q[:, h * HEAD_PAD:(h + 1) * HEAD_PAD], k_h)
        s = jnp.where(col <= row, s, MASK_VALUE)
        p = jnp.exp2(s - jnp.max(s, axis=1, keepdims=True))
        p = p / jnp.sum(p, axis=1, keepdims=True)
        outs.append(_dot(p.astype(BF16), v_h))
    attn = jnp.concatenate(outs, axis=1)
    y = (attn * _silu(gate)).astype(BF16)
    h1 = x + _dot(y, wao_ref[...])

    hn1 = _rmsnorm(h1, bg_ref[...]).astype(BF16)
    u = _dot(hn1, wbin_ref[:, :LRU_WIDTH])
    trow = lax.broadcasted_iota(jnp.int32, (N_META, LRU_WIDTH), 0)
    uc = cb_ref[...] + u * cw_ref[CONV_WIDTH - 1:CONV_WIDTH, :]
    for k in range(1, CONV_WIDTH):
        shifted = jnp.where(trow >= k, pltpu.roll(u, k, 0), 0.0)
        uc = uc + shifted * cw_ref[CONV_WIDTH - 1 - k:CONV_WIDTH - k, :]
    a, mult, iu = _lru_gates(uc, wgate_ref, brg_ref[...], big_ref[...], lam_ref[...])
    b = jnp.where(trow == 0, 1.0, mult) * iu
    h = jnp.zeros((1, LRU_WIDTH), F32)
    for t in range(N_META):
        h = a[t:t + 1, :] * h + b[t:t + 1, :]
    h0_ref[...] = h
    utail_ref[...] = u[N_META - CARRY_ROWS:, :]


def _proj_kernel(x_ref, ag_ref, w1_ref, qg_ref, kvg_ref, wuq_ref, wukv_ref, c_ref, sa_ref, sb_ref,
                 q_ref, kn_ref, v_ref, kr_ref):
    hn = _rmsnorm(x_ref[...], ag_ref[...]).astype(BF16)
    q, k_nope, v, k_rope = _mla_project(hn, w1_ref[...], qg_ref[...], kvg_ref[...], wuq_ref[...],
                                         wukv_ref[...], c_ref[...], sa_ref[...], sb_ref[...])
    q_ref[...] = q
    kn_ref[...] = k_nope
    v_ref[...] = v
    kr_ref[...] = k_rope


def _attn_kernel(q_ref, kn_ref, kr_ref, v_ref, kmeta_ref, vmeta_ref, o_ref, kcat_ref, vext_ref):
    kcat_ref[:, :QK_NOPE] = kn_ref[...]
    kcat_ref[:, QK_NOPE:] = kr_ref[...]
    vext_ref[:, :V_HEAD] = v_ref[...]
    vext_ref[:, V_HEAD:] = jnp.ones((vext_ref.shape[0], SUM_COLS), vext_ref.dtype)
    n_q = q_ref.shape[0] // ATTN_Q
    row = lax.broadcasted_iota(jnp.int32, (ATTN_Q, ATTN_K), 0)
    col = lax.broadcasted_iota(jnp.int32, (ATTN_Q, ATTN_K), 1)
    diag_ok = col <= row
    meta_ok = lax.broadcasted_iota(jnp.int32, (ATTN_Q, META_PAD), 1) < N_META
    def tile(qi):
        q = q_ref[qi * ATTN_Q:(qi + 1) * ATTN_Q, :]
        s = jnp.where(diag_ok, _dot_nt(q, kcat_ref[qi * ATTN_K:(qi + 1) * ATTN_K, :]), MASK_VALUE)
        sm = jnp.where(meta_ok, _dot_nt(q, kmeta_ref[...]), MASK_VALUE)
        m = jnp.broadcast_to(jnp.maximum(jnp.max(s, axis=1, keepdims=True),
                                         jnp.max(sm, axis=1, keepdims=True)), (ATTN_Q, LANES))
        p = jnp.exp2(s - jnp.concatenate([m] * (ATTN_K // LANES), axis=1)).astype(BF16)
        acc = (_dot(p, vext_ref[qi * ATTN_K:(qi + 1) * ATTN_K, :])
               + _dot(jnp.exp2(sm - m).astype(BF16), vmeta_ref[...]))
        yield
        for j in range(qi):
            s = _dot_nt(q, kcat_ref[j * ATTN_K:(j + 1) * ATTN_K, :])
            m_next = jnp.maximum(m, jnp.max(s, axis=1, keepdims=True))
            p = jnp.exp2(s - jnp.concatenate([m_next] * (ATTN_K // LANES), axis=1)).astype(BF16)
            alpha = jnp.exp2(m - m_next)
            acc = (jnp.concatenate([alpha, alpha], axis=1) * acc
                   + _dot(p, vext_ref[j * ATTN_K:(j + 1) * ATTN_K, :]))
            m = m_next
            yield
        out = acc[:, :V_HEAD] / acc[:, V_HEAD:]
        o_ref[qi * ATTN_Q:(qi + 1) * ATTN_Q, :] = out.astype(o_ref.dtype)

    _run_round_robin([tile(i) for lo in range(n_q // 2) for i in (lo, n_q - 1 - lo)])


def _tail_kernel(x_ref, attn_ref, ag_ref, wg_ref, wao_ref, bg_ref, wbin_ref, cw_ref, cb_ref,
                 wgate_ref, brg_ref, big_ref, lam_ref, wbo_ref, fg_ref, utail_ref, h0_ref,
                 o_ref, uext_ref, a_ref, b_ref, hs_ref, hcar_ref):
    ti = pl.program_id(0)
    nb, tt, d = x_ref.shape
    rows = nb * tt
    carry = (CONV_WIDTH - 1) * nb

    @pl.when(ti == 0)
    def _():
        for g in range(N_SLAB):
            lanes = slice(g * LANES, (g + 1) * LANES)
            for j in range(CONV_WIDTH - 1):
                src = CARRY_ROWS - (CONV_WIDTH - 1) + j
                uext_ref[g, j * nb:(j + 1) * nb, :] = jnp.broadcast_to(utail_ref[src:src + 1, lanes], (nb, LANES))
            hcar_ref[g] = jnp.broadcast_to(h0_ref[:, lanes], (nb, LANES))

    x = x_ref[...].reshape(rows, d)
    hn = _rmsnorm(x, ag_ref[...]).astype(BF16)
    gate = _dot(hn, wg_ref[...])
    y = (attn_ref[...].reshape(rows, d).astype(F32) * _silu(gate)).astype(BF16)
    h1 = x + _dot(y, wao_ref[...])

    hn1 = _rmsnorm(h1, bg_ref[...]).astype(BF16)
    proj = _dot(hn1, wbin_ref[...])
    u = proj[:, :LRU_WIDTH]
    gate_b = proj[:, LRU_WIDTH:]
    for g in range(N_SLAB):
        for b in range(nb):
            uext_ref[g, pl.ds(carry + b, tt, stride=nb), :] = u[b * tt:(b + 1) * tt, g * LANES:(g + 1) * LANES]

    for blk in range(LRU_BLOCKS):
        ucs = []
        for half in range(LRU_BLOCK // LANES):
            g = blk * (LRU_BLOCK // LANES) + half
            lanes = slice(g * LANES, (g + 1) * LANES)
            uc = cb_ref[:, lanes]
            for k in range(CONV_WIDTH):
                uc = uc + uext_ref[g, carry - k * nb:carry - k * nb + rows, :] * cw_ref[CONV_WIDTH - 1 - k:CONV_WIDTH - k, lanes]
            ucs.append(uc)
        ri = _dot(jnp.concatenate(ucs, axis=1).astype(BF16), wgate_ref[blk])
        for half in range(LRU_BLOCK // LANES):
            g = blk * (LRU_BLOCK // LANES) + half
            lanes = slice(g * LANES, (g + 1) * LANES)
            a, mult, iu = _lru_terms(ri[:, half * LANES:(half + 1) * LANES],
                                     ri[:, LRU_BLOCK + half * LANES:LRU_BLOCK + (half + 1) * LANES],
                                     ucs[half], brg_ref[:, lanes], big_ref[:, lanes], lam_ref[:, lanes])
            a_ref[g] = a
            b_ref[g] = mult * iu
    for g in range(N_SLAB):
        uext_ref[g, :carry, :] = uext_ref[g, rows:rows + carry, :]

    def scan(t, hs):
        r = pl.multiple_of(t * nb, nb)
        new = []
        for g in range(N_SLAB):
            h = a_ref[g, pl.ds(r, nb), :] * hs[g] + b_ref[g, pl.ds(r, nb), :]
            hs_ref[g, pl.ds(r, nb), :] = h
            new.append(h)
        return tuple(new)

    h_last = lax.fori_loop(0, tt, scan, tuple(hcar_ref[g] for g in range(N_SLAB)), unroll=8)
    for g in range(N_SLAB):
        hcar_ref[g] = h_last[g]

    hs = jnp.concatenate(
        [jnp.concatenate([hs_ref[g, pl.ds(b, tt, stride=nb), :] for b in range(nb)], axis=0)
         for g in range(N_SLAB)], axis=1)
    y2 = (hs * _silu(gate_b)).astype(BF16)
    h2 = h1 + _dot(y2, wbo_ref[...])
    o_ref[...] = _rmsnorm(h2, fg_ref[...]).reshape(nb, tt, d)


def _const_spec(shape):
    zeros = (0,) * len(shape)
    return pl.BlockSpec(shape, lambda *_: zeros, pipeline_mode=pl.Buffered(1))


def _rope_tables(start, n_pos):
    pos = jnp.arange(start, start + n_pos, dtype=F32)
    inv_freq = ROPE_BASE ** (-jnp.arange(0, QK_ROPE, 2, dtype=F32) / QK_ROPE)
    ang = pos[:, None] * inv_freq[None, :]
    cos, sin = jnp.cos(ang), jnp.sin(ang)
    z = jnp.zeros_like(cos)
    c = jnp.concatenate([cos, cos, z, z], axis=1)
    sa = jnp.concatenate([-sin, z, z, z], axis=1)
    sb = jnp.concatenate([z, sin, z, z], axis=1)
    return c, sa, sb


def kernel(x, meta_tokens, a_norm_g, a_w_in, a_q_norm_g, a_kv_norm_g, a_w_uq, a_w_ukv, a_w_out,
           b_norm_g, b_w_in, b_conv_w, b_conv_b, b_w_rg, b_b_rg, b_w_ig, b_b_ig, b_lam, b_w_out,
           final_norm_g):
    B, T, D = x.shape
    assert D == D_MODEL and T % ATTN_Q == 0 and T % PROJ_ROWS == 0 and T % TAIL_T == 0
    assert B == SUBLANES
    assert ATTN_Q == ATTN_K and (T // ATTN_Q) % 2 == 0
    assert a_norm_g.shape[0] == 1 and b_norm_g.shape[0] == 1

    w_in = a_w_in[0]
    n_lat = Q_LORA + KV_LORA + QK_ROPE
    w1 = jnp.concatenate([w_in[:, :n_lat], jnp.zeros((D, ROPE_LANES - QK_ROPE), F32)], axis=1).astype(BF16)
    wg = w_in[:, n_lat:].astype(BF16)
    wuq = a_w_uq[0].reshape(Q_LORA, HEADS, QK_NOPE + QK_ROPE) * ((QK_NOPE + QK_ROPE) ** -0.5 * LOG2E)
    wuq = jnp.pad(wuq, ((0, 0), (0, 0), (0, HEAD_PAD - QK_NOPE - QK_ROPE)))
    wuq = wuq.reshape(Q_LORA, HEADS * HEAD_PAD).astype(BF16)
    wukv = a_w_ukv[0].reshape(KV_LORA, HEADS, QK_NOPE + V_HEAD)
    wukv = jnp.concatenate([wukv[:, :, :QK_NOPE].reshape(KV_LORA, HEADS * QK_NOPE),
                            wukv[:, :, QK_NOPE:].reshape(KV_LORA, HEADS * V_HEAD)], axis=1).astype(BF16)
    wao = a_w_out[0].astype(BF16)
    wbin = b_w_in[0].astype(BF16)
    wgate = (0.5 * jnp.concatenate([b_w_rg[0], b_w_ig[0]], axis=2)).astype(BF16)
    wbo = b_w_out[0].astype(BF16)
    row = lambda v: v.reshape(1, -1)
    ag, qg, kvg, bg, fg = row(a_norm_g[0]), row(a_q_norm_g[0]), row(a_kv_norm_g[0]), row(b_norm_g[0]), row(final_norm_g)
    cw, cb = b_conv_w[0], row(b_conv_b[0])
    brg, big, lam = row(b_b_rg[0]), row(b_b_ig[0]), row(b_lam[0])
    c_m, sa_m, sb_m = _rope_tables(0, N_META)
    c_r, sa_r, sb_r = _rope_tables(N_META, T)

    params = functools.partial(pltpu.CompilerParams, vmem_limit_bytes=VMEM_LIMIT_BYTES)

    kmeta, vmeta, utail, h0 = pl.pallas_call(
        _meta_kernel,
        out_shape=(jax.ShapeDtypeStruct((HEADS, META_PAD, HEAD_PAD), BF16),
                   jax.ShapeDtypeStruct((HEADS, META_PAD, V_HEAD + SUM_COLS), BF16),
                   jax.ShapeDtypeStruct((CARRY_ROWS, LRU_WIDTH), F32),
                   jax.ShapeDtypeStruct((1, LRU_WIDTH), F32)),
        compiler_params=params(),
        name="meta",
    )(meta_tokens, ag, w1, wg, qg, kvg, wuq, wukv, wao, bg, wbin, cw, cb, wgate,
      brg, big, lam, c_m, sa_m, sb_m)

    n_p = T // PROJ_ROWS
    tile_p = lambda w: pl.BlockSpec((None, PROJ_ROWS, w), lambda b, i: (b, i, 0))
    tab_p = pl.BlockSpec((PROJ_ROWS, ROPE_LANES), lambda b, i: (i, 0))
    cs = _const_spec
    q, k_nope, v, k_rope = pl.pallas_call(
        _proj_kernel,
        grid=(B, n_p),
        in_specs=[tile_p(D), cs((1, D)), cs((D, PROJ_COLS)), cs((1, Q_LORA)), cs((1, KV_LORA)),
                  cs((Q_LORA, HEADS * HEAD_PAD)), cs((KV_LORA, HEADS * (QK_NOPE + V_HEAD))),
                  tab_p, tab_p, tab_p],
        out_specs=(tile_p(HEADS * HEAD_PAD), tile_p(HEADS * QK_NOPE), tile_p(HEADS * V_HEAD), tile_p(ROPE_LANES)),
        out_shape=(jax.ShapeDtypeStruct((B, T, HEADS * HEAD_PAD), BF16),
                   jax.ShapeDtypeStruct((B, T, HEADS * QK_NOPE), BF16),
                   jax.ShapeDtypeStruct((B, T, HEADS * V_HEAD), BF16),
                   jax.ShapeDtypeStruct((B, T, ROPE_LANES), BF16)),
        compiler_params=params(dimension_semantics=("parallel", "parallel")),
        name="proj",
    )(x, ag, w1, qg, kvg, wuq, wukv, c_r, sa_r, sb_r)

    attn = pl.pallas_call(
        _attn_kernel,
        grid=(B, HEADS),
        in_specs=[pl.BlockSpec((None, T, HEAD_PAD), lambda b, h: (b, 0, h)),
                  pl.BlockSpec((None, T, QK_NOPE), lambda b, h: (b, 0, h)),
                  pl.BlockSpec((None, T, ROPE_LANES), lambda b, h: (b, 0, 0)),
                  pl.BlockSpec((None, T, V_HEAD), lambda b, h: (b, 0, h)),
                  pl.BlockSpec((None, META_PAD, HEAD_PAD), lambda b, h: (h, 0, 0)),
                  pl.BlockSpec((None, META_PAD, V_HEAD + SUM_COLS), lambda b, h: (h, 0, 0))],
        out_specs=pl.BlockSpec((None, T, V_HEAD), lambda b, h: (b, 0, h)),
        out_shape=jax.ShapeDtypeStruct((B, T, HEADS * V_HEAD), BF16),
        scratch_shapes=[pltpu.VMEM((T, HEAD_PAD), BF16),
                        pltpu.VMEM((T, V_HEAD + SUM_COLS), BF16)],
        compiler_params=params(dimension_semantics=("parallel", "parallel")),
        name="attn",
    )(q, k_nope, k_rope, v, kmeta, vmeta)

    n_t = T // TAIL_T
    tail_rows = B * TAIL_T
    tile_t = pl.BlockSpec((B, TAIL_T, D), lambda i: (0, i, 0))
    out = pl.pallas_call(
        _tail_kernel,
        grid=(n_t,),
        in_specs=[tile_t, tile_t, cs((1, D)), cs((D, D)), cs((D, D)), cs((1, D)), cs((D, 2 * LRU_WIDTH)),
                  cs((CONV_WIDTH, LRU_WIDTH)), cs((1, LRU_WIDTH)),
                  cs((LRU_BLOCKS, LRU_BLOCK, 2 * LRU_BLOCK)), cs((1, LRU_WIDTH)), cs((1, LRU_WIDTH)),
                  cs((1, LRU_WIDTH)), cs((LRU_WIDTH, D)), cs((1, D)),
                  cs((CARRY_ROWS, LRU_WIDTH)), cs((1, LRU_WIDTH))],
        out_specs=tile_t,
        out_shape=jax.ShapeDtypeStruct((B, T, D), x.dtype),
        scratch_shapes=[pltpu.VMEM((N_SLAB, (CONV_WIDTH - 1) * B + tail_rows, LANES), F32),
                        pltpu.VMEM((N_SLAB, tail_rows, LANES), F32),
                        pltpu.VMEM((N_SLAB, tail_rows, LANES), F32),
                        pltpu.VMEM((N_SLAB, tail_rows, LANES), F32),
                        pltpu.VMEM((N_SLAB, B, LANES), F32)],
        compiler_params=params(dimension_semantics=("arbitrary",)),
        name="tail",
    )(x, attn, ag, wg, wao, bg, wbin, cw, cb, wgate, brg, big, lam, wbo, fg, utail, h0)
    return out
```

```python
import functools
import math

import jax
import jax.numpy as jnp
from jax import lax
from jax.experimental import pallas as pl
from jax.experimental.pallas import tpu as pltpu

F32 = jnp.float32
BF16 = jnp.bfloat16

D_MODEL = 1024
N_META = 16
RMS_EPS = 1e-6
HEADS = 8
QK_NOPE = 128
QK_ROPE = 64
V_HEAD = 128
Q_LORA = 384
KV_LORA = 256
ROPE_BASE = 10000.0
HEAD_PAD = 256
ROPE_LANES = 128
META_PAD = 128
SUM_COLS = 128
PROJ_COLS = Q_LORA + KV_LORA + ROPE_LANES
LRU_WIDTH = 1024
LRU_BLOCKS = 4
LRU_BLOCK = 256
CONV_WIDTH = 4
LRU_C = 8.0
MASK_VALUE = -1e30
LOG2E = math.log2(math.e)

VMEM_LIMIT_BYTES = 56 * 1024 * 1024

PROJ_ROWS = 1024
ATTN_Q = 512
ATTN_K = 256
TAIL_T = 64
CARRY_ROWS = 8
LANES = 128
SUBLANES = 8
N_SLAB = LRU_WIDTH // LANES


def _rmsnorm(x, g):
    ms = jnp.mean(x * x, axis=-1, keepdims=True)
    return (x * lax.rsqrt(ms + RMS_EPS)) * g


def _rope128(x, c, sa, sb):
    return x * c + pltpu.roll(x, 96, 1) * sa + pltpu.roll(x, 32, 1) * sb


_DONE = object()


def _run_round_robin(gens):
    live = list(gens)
    while live:
        live = [g for g in live if next(g, _DONE) is not _DONE]


def _silu(x):
    hx = 0.5 * x
    return hx * jnp.tanh(hx) + hx


def _dot(a, b):
    return jnp.dot(a, b, preferred_element_type=F32)


def _dot_nt(a, b):
    return lax.dot_general(a, b, (((1,), (1,)), ((), ())), preferred_element_type=F32)


def _mla_project(hn_bf16, w1, qg, kvg, wuq, wukv, c, sa, sb):
    proj = _dot(hn_bf16, w1)
    qln = _rmsnorm(proj[:, :Q_LORA], qg).astype(BF16)
    kvn = _rmsnorm(proj[:, Q_LORA:Q_LORA + KV_LORA], kvg).astype(BF16)
    qf = _dot(qln, wuq)
    parts = []
    for h in range(HEADS):
        parts.append(qf[:, h * HEAD_PAD:h * HEAD_PAD + QK_NOPE])
        parts.append(_rope128(qf[:, h * HEAD_PAD + QK_NOPE:(h + 1) * HEAD_PAD], c, sa, sb))
    q = jnp.concatenate(parts, axis=1).astype(BF16)
    kv = _dot(kvn, wukv)
    k_nope = kv[:, :HEADS * QK_NOPE].astype(BF16)
    v = kv[:, HEADS * QK_NOPE:].astype(BF16)
    k_rope = _rope128(proj[:, Q_LORA + KV_LORA:], c, sa, sb).astype(BF16)
    return q, k_nope, v, k_rope


def _lru_terms(r_half, i_half, uc, b_rg, b_ig, lam):
    tr = jnp.tanh(r_half + 0.5 * b_rg)
    ti = jnp.tanh(i_half + 0.5 * b_ig)
    neg_lam = -lam
    softplus = jnp.maximum(neg_lam, 0.0) + jnp.log1p(jnp.exp(-jnp.abs(neg_lam)))
    half_c = (0.5 * LRU_C) * softplus
    z = half_c * tr + half_c
    a = jnp.exp2(z * (-LOG2E))
    t = jnp.tanh(z) * (1.0 + a * a)
    mult = jnp.where(t > 0.0, t * lax.rsqrt(t), 0.0)
    hu = 0.5 * uc
    return a, mult, hu * ti + hu


def _lru_gates(uc, wgate_ref, b_rg, b_ig, lam):
    rs, is_ = [], []
    for g in range(LRU_BLOCKS):
        ri = _dot(uc[:, g * LRU_BLOCK:(g + 1) * LRU_BLOCK].astype(BF16), wgate_ref[g])
        rs.append(ri[:, :LRU_BLOCK])
        is_.append(ri[:, LRU_BLOCK:])
    return _lru_terms(jnp.concatenate(rs, axis=1), jnp.concatenate(is_, axis=1), uc, b_rg, b_ig, lam)


def _meta_kernel(meta_ref, ag_ref, w1_ref, wg_ref, qg_ref, kvg_ref, wuq_ref, wukv_ref, wao_ref,
                 bg_ref, wbin_ref, cw_ref, cb_ref, wgate_ref, brg_ref, big_ref, lam_ref,
                 c_ref, sa_ref, sb_ref,
                 kmeta_ref, vmeta_ref, utail_ref, h0_ref):
    x = meta_ref[...]
    hn = _rmsnorm(x, ag_ref[...]).astype(BF16)
    q, k_nope, v, k_rope = _mla_project(hn, w1_ref[...], qg_ref[...], kvg_ref[...], wuq_ref[...],
                                         wukv_ref[...], c_ref[...], sa_ref[...], sb_ref[...])
    gate = _dot(hn, wg_ref[...])
    row = lax.broadcasted_iota(jnp.int32, (N_META, N_META), 0)
    col = lax.broadcasted_iota(jnp.int32, (N_META, N_META), 1)
    kmeta_ref[...] = jnp.zeros(kmeta_ref.shape, kmeta_ref.dtype)
    vmeta_ref[:, :, :V_HEAD] = jnp.zeros((HEADS, META_PAD, V_HEAD), vmeta_ref.dtype)
    vmeta_ref[:, :, V_HEAD:] = jnp.ones((HEADS, META_PAD, SUM_COLS), vmeta_ref.dtype)
    outs = []
    for h in range(HEADS):
        k_h = jnp.concatenate([k_nope[:, h * QK_NOPE:(h + 1) * QK_NOPE], k_rope], axis=1)
        v_h = v[:, h * V_HEAD:(h + 1) * V_HEAD]
        kmeta_ref[h, :N_META, :] = k_h
        vmeta_ref[h, :N_META, :V_HEAD] = v_h
        s = _dot_nt(q[:, h * HEAD_PAD:(h + 1) * HEAD_PAD], k_h)
        s = jnp.where(col <= row, s, MASK_VALUE)
        p = jnp.exp2(s - jnp.max(s, axis=1, keepdims=True))
        p = p / jnp.sum(p, axis=1, keepdims=True)
        outs.append(_dot(p.astype(BF16), v_h))
    attn = jnp.concatenate(outs, axis=1)
    y = (attn * _silu(gate)).astype(BF16)
    h1 = x + _dot(y, wao_ref[...])

    hn1 = _rmsnorm(h1, bg_ref[...]).astype(BF16)
    u = _dot(hn1, wbin_ref[:, :LRU_WIDTH])
    trow = lax.broadcasted_iota(jnp.int32, (N_META, LRU_WIDTH), 0)
    uc = cb_ref[...] + u * cw_ref[CONV_WIDTH - 1:CONV_WIDTH, :]
    for k in range(1, CONV_WIDTH):
        shifted = jnp.where(trow >= k, pltpu.roll(u, k, 0), 0.0)
        uc = uc + shifted * cw_ref[CONV_WIDTH - 1 - k:CONV_WIDTH - k, :]
    a, mult, iu = _lru_gates(uc, wgate_ref, brg_ref[...], big_ref[...], lam_ref[...])
    b = jnp.where(trow == 0, 1.0, mult) * iu
    h = jnp.zeros((1, LRU_WIDTH), F32)
    for t in range(N_META):
        h = a[t:t + 1, :] * h + b[t:t + 1, :]
    h0_ref[...] = h
    utail_ref[...] = u[N_META - CARRY_ROWS:, :]


def _proj_kernel(x_ref, ag_ref, w1_ref, qg_ref, kvg_ref, wuq_ref, wukv_ref, c_ref, sa_ref, sb_ref,
                 q_ref, kn_ref, v_ref, kr_ref):
    hn = _rmsnorm(x_ref[...], ag_ref[...]).astype(BF16)
    q, k_nope, v, k_rope = _mla_project(hn, w1_ref[...], qg_ref[...], kvg_ref[...], wuq_ref[...],
                                         wukv_ref[...], c_ref[...], sa_ref[...], sb_ref[...])
    q_ref[...] = q
    kn_ref[...] = k_nope
    v_ref[...] = v
    kr_ref[...] = k_rope


def _attn_kernel(q_ref, kn_ref, kr_ref, v_ref, kmeta_ref, vmeta_ref, o_ref, kcat_ref, vext_ref):
    kcat_ref[:, :QK_NOPE] = kn_ref[...]
    kcat_ref[:, QK_NOPE:] = kr_ref[...]
    vext_ref[:, :V_HEAD] = v_ref[...]
    vext_ref[:, V_HEAD:] = jnp.ones((vext_ref.shape[0], SUM_COLS), vext_ref.dtype)
    n_q = q_ref.shape[0] // ATTN_Q
    per_q = ATTN_Q // ATTN_K
    row = lax.broadcasted_iota(jnp.int32, (ATTN_Q, ATTN_K), 0)
    col = lax.broadcasted_iota(jnp.int32, (ATTN_Q, ATTN_K), 1)
    meta_ok = lax.broadcasted_iota(jnp.int32, (ATTN_Q, META_PAD), 1) < N_META
    def tile(qi):
        q = q_ref[qi * ATTN_Q:(qi + 1) * ATTN_Q, :]
        j0 = qi * per_q
        s = jnp.where(col <= row, _dot_nt(q, kcat_ref[j0 * ATTN_K:(j0 + 1) * ATTN_K, :]), MASK_VALUE)
        sm = jnp.where(meta_ok, _dot_nt(q, kmeta_ref[...]), MASK_VALUE)
        m = jnp.broadcast_to(jnp.maximum(jnp.max(s, axis=1, keepdims=True),
                                         jnp.max(sm, axis=1, keepdims=True)), (ATTN_Q, LANES))
        p = jnp.exp2(s - jnp.concatenate([m] * (ATTN_K // LANES), axis=1)).astype(BF16)
        acc = (_dot(p, vext_ref[j0 * ATTN_K:(j0 + 1) * ATTN_K, :])
               + _dot(jnp.exp2(sm - m).astype(BF16), vmeta_ref[...]))
        yield
        for j in list(range(j0 + 1, j0 + per_q)) + list(range(j0)):
            s = _dot_nt(q, kcat_ref[j * ATTN_K:(j + 1) * ATTN_K, :])
            if j > j0:
                s = jnp.where(col + (j - j0) * ATTN_K <= row, s, MASK_VALUE)
            m_next = jnp.maximum(m, jnp.max(s, axis=1, keepdims=True))
            p = jnp.exp2(s - jnp.concatenate([m_next] * (ATTN_K // LANES), axis=1)).astype(BF16)
            alpha = jnp.exp2(m - m_next)
            acc = (jnp.concatenate([alpha, alpha], axis=1) * acc
                   + _dot(p, vext_ref[j * ATTN_K:(j + 1) * ATTN_K, :]))
            m = m_next
            yield
        out = acc[:, :V_HEAD] / acc[:, V_HEAD:]
        o_ref[qi * ATTN_Q:(qi + 1) * ATTN_Q, :] = out.astype(o_ref.dtype)

    _run_round_robin([tile(i) for lo in range(n_q // 2) for i in (lo, n_q - 1 - lo)])


def _tail_kernel(x_ref, attn_ref, ag_ref, wg_ref, wao_ref, bg_ref, wbin_ref, cw_ref, cb_ref,
                 wgate_ref, brg_ref, big_ref, lam_ref, wbo_ref, fg_ref, utail_ref, h0_ref,
                 o_ref, uext_ref, a_ref, b_ref, hs_ref, hcar_ref):
    ti = pl.program_id(0)
    nb, tt, d = x_ref.shape
    rows = nb * tt
    carry = (CONV_WIDTH - 1) * nb

    @pl.when(ti == 0)
    def _():
        for g in range(N_SLAB):
            lanes = slice(g * LANES, (g + 1) * LANES)
            for j in range(CONV_WIDTH - 1):
                src = CARRY_ROWS - (CONV_WIDTH - 1) + j
                uext_ref[g, j * nb:(j + 1) * nb, :] = jnp.broadcast_to(utail_ref[src:src + 1, lanes], (nb, LANES))
            hcar_ref[g] = jnp.broadcast_to(h0_ref[:, lanes], (nb, LANES))

    x = x_ref[...].reshape(rows, d)
    hn = _rmsnorm(x, ag_ref[...]).astype(BF16)
    gate = _dot(hn, wg_ref[...])
    y = (attn_ref[...].reshape(rows, d).astype(F32) * _silu(gate)).astype(BF16)
    h1 = x + _dot(y, wao_ref[...])

    hn1 = _rmsnorm(h1, bg_ref[...]).astype(BF16)
    proj = _dot(hn1, wbin_ref[...])
    u = proj[:, :LRU_WIDTH]
    gate_b = proj[:, LRU_WIDTH:]
    for g in range(N_SLAB):
        for b in range(nb):
            uext_ref[g, pl.ds(carry + b, tt, stride=nb), :] = u[b * tt:(b + 1) * tt, g * LANES:(g + 1) * LANES]

    for blk in range(LRU_BLOCKS):
        ucs = []
        for half in range(LRU_BLOCK // LANES):
            g = blk * (LRU_BLOCK // LANES) + half
            lanes = slice(g * LANES, (g + 1) * LANES)
            uc = cb_ref[:, lanes]
            for k in range(CONV_WIDTH):
                uc = uc + uext_ref[g, carry - k * nb:carry - k * nb + rows, :] * cw_ref[CONV_WIDTH - 1 - k:CONV_WIDTH - k, lanes]
            ucs.append(uc)
        ri = _dot(jnp.concatenate(ucs, axis=1).astype(BF16), wgate_ref[blk])
        for half in range(LRU_BLOCK // LANES):
            g = blk * (LRU_BLOCK // LANES) + half
            lanes = slice(g * LANES, (g + 1) * LANES)
            a, mult, iu = _lru_terms(ri[:, half * LANES:(half + 1) * LANES],
                                     ri[:, LRU_BLOCK + half * LANES:LRU_BLOCK + (half + 1) * LANES],
                                     ucs[half], brg_ref[:, lanes], big_ref[:, lanes], lam_ref[:, lanes])
            a_ref[g] = a
            b_ref[g] = mult * iu
    for g in range(N_SLAB):
        uext_ref[g, :carry, :] = uext_ref[g, rows:rows + carry, :]

    def scan(t, hs):
        r = pl.multiple_of(t * nb, nb)
        new = []
        for g in range(N_SLAB):
            h = a_ref[g, pl.ds(r, nb), :] * hs[g] + b_ref[g, pl.ds(r, nb), :]
            hs_ref[g, pl.ds(r, nb), :] = h
            new.append(h)
        return tuple(new)

    h_last = lax.fori_loop(0, tt, scan, tuple(hcar_ref[g] for g in range(N_SLAB)), unroll=8)
    for g in range(N_SLAB):
        hcar_ref[g] = h_last[g]

    hs = jnp.concatenate(
        [jnp.concatenate([hs_ref[g, pl.ds(b, tt, stride=nb), :] for b in range(nb)], axis=0)
         for g in range(N_SLAB)], axis=1)
    y2 = (hs * _silu(gate_b)).astype(BF16)
    h2 = h1 + _dot(y2, wbo_ref[...])
    o_ref[...] = _rmsnorm(h2, fg_ref[...]).reshape(nb, tt, d)


def _const_spec(shape):
    zeros = (0,) * len(shape)
    return pl.BlockSpec(shape, lambda *_: zeros, pipeline_mode=pl.Buffered(1))


def _rope_tables(start, n_pos):
    pos = jnp.arange(start, start + n_pos, dtype=F32)
    inv_freq = ROPE_BASE ** (-jnp.arange(0, QK_ROPE, 2, dtype=F32) / QK_ROPE)
    ang = pos[:, None] * inv_freq[None, :]
    cos, sin = jnp.cos(ang), jnp.sin(ang)
    z = jnp.zeros_like(cos)
    c = jnp.concatenate([cos, cos, z, z], axis=1)
    sa = jnp.concatenate([-sin, z, z, z], axis=1)
    sb = jnp.concatenate([z, sin, z, z], axis=1)
    return c, sa, sb


def kernel(x, meta_tokens, a_norm_g, a_w_in, a_q_norm_g, a_kv_norm_g, a_w_uq, a_w_ukv, a_w_out,
           b_norm_g, b_w_in, b_conv_w, b_conv_b, b_w_rg, b_b_rg, b_w_ig, b_b_ig, b_lam, b_w_out,
           final_norm_g):
    B, T, D = x.shape
    assert D == D_MODEL and T % ATTN_Q == 0 and T % PROJ_ROWS == 0 and T % TAIL_T == 0
    assert B == SUBLANES
    assert ATTN_Q % ATTN_K == 0 and (T // ATTN_Q) % 2 == 0
    assert a_norm_g.shape[0] == 1 and b_norm_g.shape[0] == 1

    w_in = a_w_in[0]
    n_lat = Q_LORA + KV_LORA + QK_ROPE
    w1 = jnp.concatenate([w_in[:, :n_lat], jnp.zeros((D, ROPE_LANES - QK_ROPE), F32)], axis=1).astype(BF16)
    wg = w_in[:, n_lat:].astype(BF16)
    wuq = a_w_uq[0].reshape(Q_LORA, HEADS, QK_NOPE + QK_ROPE) * ((QK_NOPE + QK_ROPE) ** -0.5 * LOG2E)
    wuq = jnp.pad(wuq, ((0, 0), (0, 0), (0, HEAD_PAD - QK_NOPE - QK_ROPE)))
    wuq = wuq.reshape(Q_LORA, HEADS * HEAD_PAD).astype(BF16)
    wukv = a_w_ukv[0].reshape(KV_LORA, HEADS, QK_NOPE + V_HEAD)
    wukv = jnp.concatenate([wukv[:, :, :QK_NOPE].reshape(KV_LORA, HEADS * QK_NOPE),
                            wukv[:, :, QK_NOPE:].reshape(KV_LORA, HEADS * V_HEAD)], axis=1).astype(BF16)
    wao = a_w_out[0].astype(BF16)
    wbin = b_w_in[0].astype(BF16)
    wgate = (0.5 * jnp.concatenate([b_w_rg[0], b_w_ig[0]], axis=2)).astype(BF16)
    wbo = b_w_out[0].astype(BF16)
    row = lambda v: v.reshape(1, -1)
    ag, qg, kvg, bg, fg = row(a_norm_g[0]), row(a_q_norm_g[0]), row(a_kv_norm_g[0]), row(b_norm_g[0]), row(final_norm_g)
    cw, cb = b_conv_w[0], row(b_conv_b[0])
    brg, big, lam = row(b_b_rg[0]), row(b_b_ig[0]), row(b_lam[0])
    c_m, sa_m, sb_m = _rope_tables(0, N_META)
    c_r, sa_r, sb_r = _rope_tables(N_META, T)

    params = functools.partial(pltpu.CompilerParams, vmem_limit_bytes=VMEM_LIMIT_BYTES)

    kmeta, vmeta, utail, h0 = pl.pallas_call(
        _meta_kernel,
        out_shape=(jax.ShapeDtypeStruct((HEADS, META_PAD, HEAD_PAD), BF16),
                   jax.ShapeDtypeStruct((HEADS, META_PAD, V_HEAD + SUM_COLS), BF16),
                   jax.ShapeDtypeStruct((CARRY_ROWS, LRU_WIDTH), F32),
                   jax.ShapeDtypeStruct((1, LRU_WIDTH), F32)),
        compiler_params=params(),
        name="meta",
    )(meta_tokens, ag, w1, wg, qg, kvg, wuq, wukv, wao, bg, wbin, cw, cb, wgate,
      brg, big, lam, c_m, sa_m, sb_m)

    n_p = T // PROJ_ROWS
    tile_p = lambda w: pl.BlockSpec((None, PROJ_ROWS, w), lambda b, i: (b, i, 0))
    tab_p = pl.BlockSpec((PROJ_ROWS, ROPE_LANES), lambda b, i: (i, 0))
    cs = _const_spec
    q, k_nope, v, k_rope = pl.pallas_call(
        _proj_kernel,
        grid=(B, n_p),
        in_specs=[tile_p(D), cs((1, D)), cs((D, PROJ_COLS)), cs((1, Q_LORA)), cs((1, KV_LORA)),
                  cs((Q_LORA, HEADS * HEAD_PAD)), cs((KV_LORA, HEADS * (QK_NOPE + V_HEAD))),
                  tab_p, tab_p, tab_p],
        out_specs=(tile_p(HEADS * HEAD_PAD), tile_p(HEADS * QK_NOPE), tile_p(HEADS * V_HEAD), tile_p(ROPE_LANES)),
        out_shape=(jax.ShapeDtypeStruct((B, T, HEADS * HEAD_PAD), BF16),
                   jax.ShapeDtypeStruct((B, T, HEADS * QK_NOPE), BF16),
                   jax.ShapeDtypeStruct((B, T, HEADS * V_HEAD), BF16),
                   jax.ShapeDtypeStruct((B, T, ROPE_LANES), BF16)),
        compiler_params=params(dimension_semantics=("parallel", "parallel")),
        name="proj",
    )(x, ag, w1, qg, kvg, wuq, wukv, c_r, sa_r, sb_r)

    attn = pl.pallas_call(
        _attn_kernel,
        grid=(B, HEADS),
        in_specs=[pl.BlockSpec((None, T, HEAD_PAD), lambda b, h: (b, 0, h)),
                  pl.BlockSpec((None, T, QK_NOPE), lambda b, h: (b, 0, h)),
                  pl.BlockSpec((None, T, ROPE_LANES), lambda b, h: (b, 0, 0)),
                  pl.BlockSpec((None, T, V_HEAD), lambda b, h: (b, 0, h)),
                  pl.BlockSpec((None, META_PAD, HEAD_PAD), lambda b, h: (h, 0, 0)),
                  pl.BlockSpec((None, META_PAD, V_HEAD + SUM_COLS), lambda b, h: (h, 0, 0))],
        out_specs=pl.BlockSpec((None, T, V_HEAD), lambda b, h: (b, 0, h)),
        out_shape=jax.ShapeDtypeStruct((B, T, HEADS * V_HEAD), BF16),
        scratch_shapes=[pltpu.VMEM((T, HEAD_PAD), BF16),
                        pltpu.VMEM((T, V_HEAD + SUM_COLS), BF16)],
        compiler_params=params(dimension_semantics=("parallel", "parallel")),
        name="attn",
    )(q, k_nope, k_rope, v, kmeta, vmeta)

    n_t = T // TAIL_T
    tail_rows = B * TAIL_T
    tile_t = pl.BlockSpec((B, TAIL_T, D), lambda i: (0, i, 0))
    out = pl.pallas_call(
        _tail_kernel,
        grid=(n_t,),
        in_specs=[tile_t, tile_t, cs((1, D)), cs((D, D)), cs((D, D)), cs((1, D)), cs((D, 2 * LRU_WIDTH)),
                  cs((CONV_WIDTH, LRU_WIDTH)), cs((1, LRU_WIDTH)),
                  cs((LRU_BLOCKS, LRU_BLOCK, 2 * LRU_BLOCK)), cs((1, LRU_WIDTH)), cs((1, LRU_WIDTH)),
                  cs((1, LRU_WIDTH)), cs((LRU_WIDTH, D)), cs((1, D)),
                  cs((CARRY_ROWS, LRU_WIDTH)), cs((1, LRU_WIDTH))],
        out_specs=tile_t,
        out_shape=jax.ShapeDtypeStruct((B, T, D), x.dtype),
        scratch_shapes=[pltpu.VMEM((N_SLAB, (CONV_WIDTH - 1) * B + tail_rows, LANES), F32),
                        pltpu.VMEM((N_SLAB, tail_rows, LANES), F32),
                        pltpu.VMEM((N_SLAB, tail_rows, LANES), F32),
                        pltpu.VMEM((N_SLAB, tail_rows, LANES), F32),
                        pltpu.VMEM((N_SLAB, B, LANES), F32)],
        compiler_params=params(dimension_semantics=("arbitrary",)),
        name="tail",
    )(x, attn, ag, wg, wao, bg, wbin, cw, cb, wgate, brg, big, lam, wbo, fg, utail, h0)
    return out
```

```python
import functools
import math

import jax
import jax.numpy as jnp
from jax import lax
from jax.experimental import pallas as pl
from jax.experimental.pallas import tpu as pltpu

F32 = jnp.float32
BF16 = jnp.bfloat16

D_MODEL = 1024
N_META = 16
RMS_EPS = 1e-6
HEADS = 8
QK_NOPE = 128
QK_ROPE = 64
V_HEAD = 128
Q_LORA = 384
KV_LORA = 256
ROPE_BASE = 10000.0
HEAD_PAD = 256
ROPE_LANES = 128
META_PAD = 128
SUM_COLS = 128
PROJ_COLS = Q_LORA + KV_LORA + ROPE_LANES
LRU_WIDTH = 1024
LRU_BLOCKS = 4
LRU_BLOCK = 256
CONV_WIDTH = 4
LRU_C = 8.0
MASK_VALUE = -1e30
LOG2E = math.log2(math.e)

VMEM_LIMIT_BYTES = 56 * 1024 * 1024

PROJ_ROWS = 1024
ATTN_Q = 256
ATTN_K = 256
TAIL_T = 64
CARRY_ROWS = 8
LANES = 128
SUBLANES = 8
N_SLAB = LRU_WIDTH // LANES


def _rmsnorm(x, g):
    ms = jnp.mean(x * x, axis=-1, keepdims=True)
    return (x * lax.rsqrt(ms + RMS_EPS)) * g


def _rope128(x, c, sa, sb):
    return x * c + pltpu.roll(x, 96, 1) * sa + pltpu.roll(x, 32, 1) * sb


_DONE = object()


def _run_round_robin(gens):
    live = list(gens)
    while live:
        live = [g for g in live if next(g, _DONE) is not _DONE]


def _silu(x):
    hx = 0.5 * x
    return hx * jnp.tanh(hx) + hx


def _dot(a, b):
    return jnp.dot(a, b, preferred_element_type=F32)


def _dot_nt(a, b):
    return lax.dot_general(a, b, (((1,), (1,)), ((), ())), preferred_element_type=F32)


def _mla_project(hn_bf16, w1, qg, kvg, wuq, wukv, c, sa, sb):
    proj = _dot(hn_bf16, w1)
    qln = _rmsnorm(proj[:, :Q_LORA], qg).astype(BF16)
    kvn = _rmsnorm(proj[:, Q_LORA:Q_LORA + KV_LORA], kvg).astype(BF16)
    qf = _dot(qln, wuq)
    parts = []
    for h in range(HEADS):
        parts.append(qf[:, h * HEAD_PAD:h * HEAD_PAD + QK_NOPE])
        parts.append(_rope128(qf[:, h * HEAD_PAD + QK_NOPE:(h + 1) * HEAD_PAD], c, sa, sb))
    q = jnp.concatenate(parts, axis=1).astype(BF16)
    kv = _dot(kvn, wukv)
    k_nope = kv[:, :HEADS * QK_NOPE].astype(BF16)
    v = kv[:, HEADS * QK_NOPE:].astype(BF16)
    k_rope = _rope128(proj[:, Q_LORA + KV_LORA:], c, sa, sb).astype(BF16)
    return q, k_nope, v, k_rope


def _lru_terms(r_half, i_half, uc, b_rg, b_ig, lam):
    tr = jnp.tanh(r_half + 0.5 * b_rg)
    ti = jnp.tanh(i_half + 0.5 * b_ig)
    neg_lam = -lam
    softplus = jnp.maximum(neg_lam, 0.0) + jnp.log1p(jnp.exp(-jnp.abs(neg_lam)))
    half_c = (0.5 * LRU_C) * softplus
    z = half_c * tr + half_c
    a = jnp.exp2(z * (-LOG2E))
    t = jnp.tanh(z) * (1.0 + a * a)
    mult = jnp.where(t > 0.0, t * lax.rsqrt(t), 0.0)
    hu = 0.5 * uc
    return a, mult, hu * ti + hu


def _lru_gates(uc, wgate_ref, b_rg, b_ig, lam):
    rs, is_ = [], []
    for g in range(LRU_BLOCKS):
        ri = _dot(uc[:, g * LRU_BLOCK:(g + 1) * LRU_BLOCK].astype(BF16), wgate_ref[g])
        rs.append(ri[:, :LRU_BLOCK])
        is_.append(ri[:, LRU_BLOCK:])
    return _lru_terms(jnp.concatenate(rs, axis=1), jnp.concatenate(is_, axis=1), uc, b_rg, b_ig, lam)


def _meta_kernel(meta_ref, ag_ref, w1_ref, wg_ref, qg_ref, kvg_ref, wuq_ref, wukv_ref, wao_ref,
                 bg_ref, wbin_ref, cw_ref, cb_ref, wgate_ref, brg_ref, big_ref, lam_ref,
                 c_ref, sa_ref, sb_ref,
                 kmeta_ref, vmeta_ref, utail_ref, h0_ref):
    x = meta_ref[...]
    hn = _rmsnorm(x, ag_ref[...]).astype(BF16)
    q, k_nope, v, k_rope = _mla_project(hn, w1_ref[...], qg_ref[...], kvg_ref[...], wuq_ref[...],
                                         wukv_ref[...], c_ref[...], sa_ref[...], sb_ref[...])
    gate = _dot(hn, wg_ref[...])
    row = lax.broadcasted_iota(jnp.int32, (N_META, N_META), 0)
    col = lax.broadcasted_iota(jnp.int32, (N_META, N_META), 1)
    kmeta_ref[...] = jnp.zeros(kmeta_ref.shape, kmeta_ref.dtype)
    vmeta_ref[:, :, :V_HEAD] = jnp.zeros((HEADS, META_PAD, V_HEAD), vmeta_ref.dtype)
    vmeta_ref[:, :, V_HEAD:] = jnp.ones((HEADS, META_PAD, SUM_COLS), vmeta_ref.dtype)
    outs = []
    for h in range(HEADS):
        k_h = jnp.concatenate([k_nope[:, h * QK_NOPE:(h + 1) * QK_NOPE], k_rope], axis=1)
        v_h = v[:, h * V_HEAD:(h + 1) * V_HEAD]
        kmeta_ref[h, :N_META, :] = k_h
        vmeta_ref[h, :N_META, :V_HEAD] = v_h
        s = _dot_nt(q[:, h * HEAD_PAD:(h + 1) * HEAD_PAD], k_h)
        s = jnp.where(col <= row, s, MASK_VALUE)
        p = jnp.exp2(s - jnp.max(s, axis=1, keepdims=True))
        p = p / jnp.sum(p, axis=1, keepdims=True)
        outs.append(_dot(p.astype(BF16), v_h))
    attn = jnp.concatenate(outs, axis=1)
    y = (attn * _silu(gate)).astype(BF16)
    h1 = x + _dot(y, wao_ref[...])

    hn1 = _rmsnorm(h1, bg_ref[...]).astype(BF16)
    u = _dot(hn1, wbin_ref[:, :LRU_WIDTH])
    trow = lax.broadcasted_iota(jnp.int32, (N_META, LRU_WIDTH), 0)
    uc = cb_ref[...] + u * cw_ref[CONV_WIDTH - 1:CONV_WIDTH, :]
    for k in range(1, CONV_WIDTH):
        shifted = jnp.where(trow >= k, pltpu.roll(u, k, 0), 0.0)
        uc = uc + shifted * cw_ref[CONV_WIDTH - 1 - k:CONV_WIDTH - k, :]
    a, mult, iu = _lru_gates(uc, wgate_ref, brg_ref[...], big_ref[...], lam_ref[...])
    b = jnp.where(trow == 0, 1.0, mult) * iu
    h = jnp.zeros((1, LRU_WIDTH), F32)
    for t in range(N_META):
        h = a[t:t + 1, :] * h + b[t:t + 1, :]
    h0_ref[...] = h
    utail_ref[...] = u[N_META - CARRY_ROWS:, :]


def _proj_kernel(x_ref, ag_ref, w1_ref, qg_ref, kvg_ref, wuq_ref, wukv_ref, c_ref, sa_ref, sb_ref,
                 q_ref, kn_ref, v_ref, kr_ref):
    hn = _rmsnorm(x_ref[...], ag_ref[...]).astype(BF16)
    q, k_nope, v, k_rope = _mla_project(hn, w1_ref[...], qg_ref[...], kvg_ref[...], wuq_ref[...],
                                         wukv_ref[...], c_ref[...], sa_ref[...], sb_ref[...])
    q_ref[...] = q
    kn_ref[...] = k_nope
    v_ref[...] = v
    kr_ref[...] = k_rope


def _attn_kernel(q_ref, kn_ref, kr_ref, v_ref, kmeta_ref, vmeta_ref, o_ref, kcat_ref, vext_ref):
    kcat_ref[:, :QK_NOPE] = kn_ref[...]
    kcat_ref[:, QK_NOPE:] = kr_ref[...]
    vext_ref[:, :V_HEAD] = v_ref[...]
    vext_ref[:, V_HEAD:] = jnp.ones((vext_ref.shape[0], SUM_COLS), vext_ref.dtype)
    n_q = q_ref.shape[0] // ATTN_Q
    per_q = ATTN_Q // ATTN_K
    row = lax.broadcasted_iota(jnp.int32, (ATTN_Q, ATTN_K), 0)
    col = lax.broadcasted_iota(jnp.int32, (ATTN_Q, ATTN_K), 1)
    meta_ok = lax.broadcasted_iota(jnp.int32, (ATTN_Q, META_PAD), 1) < N_META
    def tile(qi):
        q = q_ref[qi * ATTN_Q:(qi + 1) * ATTN_Q, :]
        j0 = qi * per_q
        s = jnp.where(col <= row, _dot_nt(q, kcat_ref[j0 * ATTN_K:(j0 + 1) * ATTN_K, :]), MASK_VALUE)
        sm = jnp.where(meta_ok, _dot_nt(q, kmeta_ref[...]), MASK_VALUE)
        m = jnp.broadcast_to(jnp.maximum(jnp.max(s, axis=1, keepdims=True),
                                         jnp.max(sm, axis=1, keepdims=True)), (ATTN_Q, LANES))
        p = jnp.exp2(s - jnp.concatenate([m] * (ATTN_K // LANES), axis=1)).astype(BF16)
        acc = (_dot(p, vext_ref[j0 * ATTN_K:(j0 + 1) * ATTN_K, :])
               + _dot(jnp.exp2(sm - m).astype(BF16), vmeta_ref[...]))
        yield
        for j in list(range(j0 + 1, j0 + per_q)) + list(range(j0)):
            s = _dot_nt(q, kcat_ref[j * ATTN_K:(j + 1) * ATTN_K, :])
            if j > j0:
                s = jnp.where(col + (j - j0) * ATTN_K <= row, s, MASK_VALUE)
            m_next = jnp.maximum(m, jnp.max(s, axis=1, keepdims=True))
            p = jnp.exp2(s - jnp.concatenate([m_next] * (ATTN_K // LANES), axis=1)).astype(BF16)
            alpha = jnp.exp2(m - m_next)
            acc = (jnp.concatenate([alpha, alpha], axis=1) * acc
                   + _dot(p, vext_ref[j * ATTN_K:(j + 1) * ATTN_K, :]))
            m = m_next
            yield
        out = acc[:, :V_HEAD] / acc[:, V_HEAD:]
        o_ref[qi * ATTN_Q:(qi + 1) * ATTN_Q, :] = out.astype(o_ref.dtype)

    _run_round_robin([tile(i) for lo in range(n_q // 2) for i in (lo, n_q - 1 - lo)])


def _tail_kernel(x_ref, attn_ref, ag_ref, wg_ref, wao_ref, bg_ref, wbin_ref, cw_ref, cb_ref,
                 wgate_ref, brg_ref, big_ref, lam_ref, wbo_ref, fg_ref, utail_ref, h0_ref,
                 o_ref, uext_ref, a_ref, b_ref, hs_ref, hcar_ref):
    ti = pl.program_id(0)
    nb, tt, d = x_ref.shape
    rows = nb * tt
    carry = (CONV_WIDTH - 1) * nb

    @pl.when(ti == 0)
    def _():
        for g in range(N_SLAB):
            lanes = slice(g * LANES, (g + 1) * LANES)
            for j in range(CONV_WIDTH - 1):
                src = CARRY_ROWS - (CONV_WIDTH - 1) + j
                uext_ref[g, j * nb:(j + 1) * nb, :] = jnp.broadcast_to(utail_ref[src:src + 1, lanes], (nb, LANES))
            hcar_ref[g] = jnp.broadcast_to(h0_ref[:, lanes], (nb, LANES))

    x = x_ref[...].reshape(rows, d)
    hn = _rmsnorm(x, ag_ref[...]).astype(BF16)
    gate = _dot(hn, wg_ref[...])
    y = (attn_ref[...].reshape(rows, d).astype(F32) * _silu(gate)).astype(BF16)
    h1 = x + _dot(y, wao_ref[...])

    hn1 = _rmsnorm(h1, bg_ref[...]).astype(BF16)
    proj = _dot(hn1, wbin_ref[...])
    u = proj[:, :LRU_WIDTH]
    gate_b = proj[:, LRU_WIDTH:]
    for g in range(N_SLAB):
        for b in range(nb):
            uext_ref[g, pl.ds(carry + b, tt, stride=nb), :] = u[b * tt:(b + 1) * tt, g * LANES:(g + 1) * LANES]

    for blk in range(LRU_BLOCKS):
        ucs = []
        for half in range(LRU_BLOCK // LANES):
            g = blk * (LRU_BLOCK // LANES) + half
            lanes = slice(g * LANES, (g + 1) * LANES)
            uc = cb_ref[:, lanes]
            for k in range(CONV_WIDTH):
                uc = uc + uext_ref[g, carry - k * nb:carry - k * nb + rows, :] * cw_ref[CONV_WIDTH - 1 - k:CONV_WIDTH - k, lanes]
            ucs.append(uc)
        ri = _dot(jnp.concatenate(ucs, axis=1).astype(BF16), wgate_ref[blk])
        for half in range(LRU_BLOCK // LANES):
            g = blk * (LRU_BLOCK // LANES) + half
            lanes = slice(g * LANES, (g + 1) * LANES)
            a, mult, iu = _lru_terms(ri[:, half * LANES:(half + 1) * LANES],
                                     ri[:, LRU_BLOCK + half * LANES:LRU_BLOCK + (half + 1) * LANES],
                                     ucs[half], brg_ref[:, lanes], big_ref[:, lanes], lam_ref[:, lanes])
            a_ref[g] = a
            b_ref[g] = mult * iu
    for g in range(N_SLAB):
        uext_ref[g, :carry, :] = uext_ref[g, rows:rows + carry, :]

    def scan(t, hs):
        r = pl.multiple_of(t * nb, nb)
        new = []
        for g in range(N_SLAB):
            h = a_ref[g, pl.ds(r, nb), :] * hs[g] + b_ref[g, pl.ds(r, nb), :]
            hs_ref[g, pl.ds(r, nb), :] = h
            new.append(h)
        return tuple(new)

    h_last = lax.fori_loop(0, tt, scan, tuple(hcar_ref[g] for g in range(N_SLAB)), unroll=8)
    for g in range(N_SLAB):
        hcar_ref[g] = h_last[g]

    hs = jnp.concatenate(
        [jnp.concatenate([hs_ref[g, pl.ds(b, tt, stride=nb), :] for b in range(nb)], axis=0)
         for g in range(N_SLAB)], axis=1)
    y2 = (hs * _silu(gate_b)).astype(BF16)
    h2 = h1 + _dot(y2, wbo_ref[...])
    o_ref[...] = _rmsnorm(h2, fg_ref[...]).reshape(nb, tt, d)


def _const_spec(shape):
    zeros = (0,) * len(shape)
    return pl.BlockSpec(shape, lambda *_: zeros, pipeline_mode=pl.Buffered(1))


def _rope_tables(start, n_pos):
    pos = jnp.arange(start, start + n_pos, dtype=F32)
    inv_freq = ROPE_BASE ** (-jnp.arange(0, QK_ROPE, 2, dtype=F32) / QK_ROPE)
    ang = pos[:, None] * inv_freq[None, :]
    cos, sin = jnp.cos(ang), jnp.sin(ang)
    z = jnp.zeros_like(cos)
    c = jnp.concatenate([cos, cos, z, z], axis=1)
    sa = jnp.concatenate([-sin, z, z, z], axis=1)
    sb = jnp.concatenate([z, sin, z, z], axis=1)
    return c, sa, sb


def kernel(x, meta_tokens, a_norm_g, a_w_in, a_q_norm_g, a_kv_norm_g, a_w_uq, a_w_ukv, a_w_out,
           b_norm_g, b_w_in, b_conv_w, b_conv_b, b_w_rg, b_b_rg, b_w_ig, b_b_ig, b_lam, b_w_out,
           final_norm_g):
    B, T, D = x.shape
    assert D == D_MODEL and T % ATTN_Q == 0 and T % PROJ_ROWS == 0 and T % TAIL_T == 0
    assert B == SUBLANES
    assert ATTN_Q % ATTN_K == 0 and (T // ATTN_Q) % 2 == 0
    assert a_norm_g.shape[0] == 1 and b_norm_g.shape[0] == 1

    w_in = a_w_in[0]
    n_lat = Q_LORA + KV_LORA + QK_ROPE
    w1 = jnp.concatenate([w_in[:, :n_lat], jnp.zeros((D, ROPE_LANES - QK_ROPE), F32)], axis=1).astype(BF16)
    wg = w_in[:, n_lat:].astype(BF16)
    wuq = a_w_uq[0].reshape(Q_LORA, HEADS, QK_NOPE + QK_ROPE) * ((QK_NOPE + QK_ROPE) ** -0.5 * LOG2E)
    wuq = jnp.pad(wuq, ((0, 0), (0, 0), (0, HEAD_PAD - QK_NOPE - QK_ROPE)))
    wuq = wuq.reshape(Q_LORA, HEADS * HEAD_PAD).astype(BF16)
    wukv = a_w_ukv[0].reshape(KV_LORA, HEADS, QK_NOPE + V_HEAD)
    wukv = jnp.concatenate([wukv[:, :, :QK_NOPE].reshape(KV_LORA, HEADS * QK_NOPE),
                            wukv[:, :, QK_NOPE:].reshape(KV_LORA, HEADS * V_HEAD)], axis=1).astype(BF16)
    wao = a_w_out[0].astype(BF16)
    wbin = b_w_in[0].astype(BF16)
    wgate = (0.5 * jnp.concatenate([b_w_rg[0], b_w_ig[0]], axis=2)).astype(BF16)
    wbo = b_w_out[0].astype(BF16)
    row = lambda v: v.reshape(1, -1)
    ag, qg, kvg, bg, fg = row(a_norm_g[0]), row(a_q_norm_g[0]), row(a_kv_norm_g[0]), row(b_norm_g[0]), row(final_norm_g)
    cw, cb = b_conv_w[0], row(b_conv_b[0])
    brg, big, lam = row(b_b_rg[0]), row(b_b_ig[0]), row(b_lam[0])
    c_m, sa_m, sb_m = _rope_tables(0, N_META)
    c_r, sa_r, sb_r = _rope_tables(N_META, T)

    params = functools.partial(pltpu.CompilerParams, vmem_limit_bytes=VMEM_LIMIT_BYTES)

    kmeta, vmeta, utail, h0 = pl.pallas_call(
        _meta_kernel,
        out_shape=(jax.ShapeDtypeStruct((HEADS, META_PAD, HEAD_PAD), BF16),
                   jax.ShapeDtypeStruct((HEADS, META_PAD, V_HEAD + SUM_COLS), BF16),
                   jax.ShapeDtypeStruct((CARRY_ROWS, LRU_WIDTH), F32),
                   jax.ShapeDtypeStruct((1, LRU_WIDTH), F32)),
        compiler_params=params(),
        name="meta",
    )(meta_tokens, ag, w1, wg, qg, kvg, wuq, wukv, wao, bg, wbin, cw, cb, wgate,
      brg, big, lam, c_m, sa_m, sb_m)

    n_p = T // PROJ_ROWS
    tile_p = lambda w: pl.BlockSpec((None, PROJ_ROWS, w), lambda b, i: (b, i, 0))
    tab_p = pl.BlockSpec((PROJ_ROWS, ROPE_LANES), lambda b, i: (i, 0))
    cs = _const_spec
    q, k_nope, v, k_rope = pl.pallas_call(
        _proj_kernel,
        grid=(B, n_p),
        in_specs=[tile_p(D), cs((1, D)), cs((D, PROJ_COLS)), cs((1, Q_LORA)), cs((1, KV_LORA)),
                  cs((Q_LORA, HEADS * HEAD_PAD)), cs((KV_LORA, HEADS * (QK_NOPE + V_HEAD))),
                  tab_p, tab_p, tab_p],
        out_specs=(tile_p(HEADS * HEAD_PAD), tile_p(HEADS * QK_NOPE), tile_p(HEADS * V_HEAD), tile_p(ROPE_LANES)),
        out_shape=(jax.ShapeDtypeStruct((B, T, HEADS * HEAD_PAD), BF16),
                   jax.ShapeDtypeStruct((B, T, HEADS * QK_NOPE), BF16),
                   jax.ShapeDtypeStruct((B, T, HEADS * V_HEAD), BF16),
                   jax.ShapeDtypeStruct((B, T, ROPE_LANES), BF16)),
        compiler_params=params(dimension_semantics=("parallel", "parallel")),
        name="proj",
    )(x, ag, w1, qg, kvg, wuq, wukv, c_r, sa_r, sb_r)

    attn = pl.pallas_call(
        _attn_kernel,
        grid=(B, HEADS),
        in_specs=[pl.BlockSpec((None, T, HEAD_PAD), lambda b, h: (b, 0, h)),
                  pl.BlockSpec((None, T, QK_NOPE), lambda b, h: (b, 0, h)),
                  pl.BlockSpec((None, T, ROPE_LANES), lambda b, h: (b, 0, 0)),
                  pl.BlockSpec((None, T, V_HEAD), lambda b, h: (b, 0, h)),
                  pl.BlockSpec((None, META_PAD, HEAD_PAD), lambda b, h: (h, 0, 0)),
                  pl.BlockSpec((None, META_PAD, V_HEAD + SUM_COLS), lambda b, h: (h, 0, 0))],
        out_specs=pl.BlockSpec((None, T, V_HEAD), lambda b, h: (b, 0, h)),
        out_shape=jax.ShapeDtypeStruct((B, T, HEADS * V_HEAD), BF16),
        scratch_shapes=[pltpu.VMEM((T, HEAD_PAD), BF16),
                        pltpu.VMEM((T, V_HEAD + SUM_COLS), BF16)],
        compiler_params=params(dimension_semantics=("parallel", "parallel")),
        name="attn",
    )(q, k_nope, k_rope, v, kmeta, vmeta)

    n_t = T // TAIL_T
    tail_rows = B * TAIL_T
    tile_t = pl.BlockSpec((B, TAIL_T, D), lambda i: (0, i, 0))
    out = pl.pallas_call(
        _tail_kernel,
        grid=(n_t,),
        in_specs=[tile_t, tile_t, cs((1, D)), cs((D, D)), cs((D, D)), cs((1, D)), cs((D, 2 * LRU_WIDTH)),
                  cs((CONV_WIDTH, LRU_WIDTH)), cs((1, LRU_WIDTH)),
                  cs((LRU_BLOCKS, LRU_BLOCK, 2 * LRU_BLOCK)), cs((1, LRU_WIDTH)), cs((1, LRU_WIDTH)),
                  cs((1, LRU_WIDTH)), cs((LRU_WIDTH, D)), cs((1, D)),
                  cs((CARRY_ROWS, LRU_WIDTH)), cs((1, LRU_WIDTH))],
        out_specs=tile_t,
        out_shape=jax.ShapeDtypeStruct((B, T, D), x.dtype),
        scratch_shapes=[pltpu.VMEM((N_SLAB, (CONV_WIDTH - 1) * B + tail_rows, LANES), F32),
                        pltpu.VMEM((N_SLAB, tail_rows, LANES), F32),
                        pltpu.VMEM((N_SLAB, tail_rows, LANES), F32),
                        pltpu.VMEM((N_SLAB, tail_rows, LANES), F32),
                        pltpu.VMEM((N_SLAB, B, LANES), F32)],
        compiler_params=params(dimension_semantics=("arbitrary",)),
        name="tail",
    )(x, attn, ag, wg, wao, bg, wbin, cw, cb, wgate, brg, big, lam, wbo, fg, utail, h0)
    return out
```
